```python
import jax, jax.numpy as jnp
from jax import lax
import numpy as np

D_MODEL = 1024
BATCH = 8
SEQ = 2048
DEPTH = 2

N_AB_LAYERS = (DEPTH + 1) // 2
N_C_LAYERS = DEPTH // 2

M_HEADS = 4
M_QK_DIM = 64
M_V_DIM = 128
M_CHUNK = 64
GATE_SOFTCAP = 15.0
G_HEADS = 4
G_K_DIM = 128
G_V_DIM = 128
G_CHUNK = 64
SHORT_CONV = 4
C_HEADS = 8
C_K_DIM = D_MODEL // C_HEADS
C_V_DIM = D_MODEL // C_HEADS
C_CHUNK = 32
D_FF = 2816
FFN_CONV = 3
EPS = 1e-6

M_QK = M_HEADS * M_QK_DIM
M_V = M_HEADS * M_V_DIM
G_QK = G_HEADS * G_K_DIM
G_V = G_HEADS * G_V_DIM
AB_SPLITS = (M_QK, M_QK, M_V, M_V, M_HEADS, M_HEADS, G_QK, G_QK, G_V, G_V, G_HEADS, G_HEADS)
AB_SPLIT_IDX = tuple(int(s) for s in np.cumsum(AB_SPLITS)[:-1])
AB_IN = sum(AB_SPLITS)
AB_MIX = M_V + G_V
C_K_TOTAL = C_HEADS * C_K_DIM
C_MIX = C_HEADS * C_V_DIM
C_IN = 2 * C_K_TOTAL + 2 * C_MIX

kernel_name = "hybrid_mlstm_gdn_hgrn2_convffn"


def rms_norm(x, gain):
    xf = x.astype(jnp.float32)
    y = xf * lax.rsqrt(jnp.mean(jnp.square(xf), -1, keepdims=True) + EPS)
    return (y * gain.astype(jnp.float32)).astype(x.dtype)


def head_rms_norm(x, gain):
    B, S, H, d = x.shape
    y = x * lax.rsqrt(jnp.mean(jnp.square(x), -1, keepdims=True) + EPS)
    return y.reshape(B, S, H * d) * gain.astype(jnp.float32)


def l2_norm(x):
    return x * lax.rsqrt(jnp.sum(jnp.square(x), -1, keepdims=True) + EPS)


def softcap(x, cap):
    return cap * jnp.tanh(x / cap)


def causal_dwconv(x, w):
    K, C = w.shape
    return lax.conv_general_dilated(
        x, w[:, None, :].astype(x.dtype), window_strides=(1,), padding=[(K - 1, 0)],
        dimension_numbers=("NWC", "WIO", "NWC"), feature_group_count=C)


def to_heads(t, n_heads):
    B, S, C = t.shape
    return t.reshape(B, S, n_heads, C // n_heads)


def to_chunks(t, L):
    B, S, H = t.shape[:3]
    t = t.reshape((B, S // L, L, H) + t.shape[3:])
    return jnp.moveaxis(t, (1, 3), (0, 2))


def from_chunks(t):
    N, B, H, L = t.shape[:4]
    t = jnp.moveaxis(t, (0, 2), (1, 3))
    return t.reshape((B, N * L, H) + t.shape[4:])


def mlstm_chunkwise(q, k, v, i_pre, f_pre):
    B, S, H, DK = q.shape
    DV = v.shape[-1]
    L = M_CHUNK
    q = q * DK ** -0.5
    qc, kc, vc = to_chunks(q, L), to_chunks(k, L), to_chunks(v, L)
    lfc = to_chunks(jax.nn.log_sigmoid(f_pre), L)
    igc = to_chunks(i_pre, L)
    causal = jnp.tril(jnp.ones((L, L), bool))

    def step(carry, inp):
        C, n, m = carry
        q_, k_, v_, lf_, ig_ = inp
        b = jnp.cumsum(lf_, -1)
        log_d = jnp.where(causal, b[..., :, None] - b[..., None, :] + ig_[..., None, :], -jnp.inf)
        log_inter = b + m[..., None]
        m_t = jnp.maximum(jnp.max(log_d, -1), log_inter)
        d = jnp.exp(log_d - m_t[..., None])
        w_inter = jnp.exp(log_inter - m_t)
        s = jnp.einsum("bhld,bhsd->bhls", q_, k_) * d
        num = jnp.einsum("bhls,bhsv->bhlv", s, v_) + w_inter[..., None] * jnp.einsum("bhld,bhdv->bhlv", q_, C)
        den = jnp.sum(s, -1) + w_inter * jnp.einsum("bhld,bhd->bhl", q_, n)
        h = num / jnp.maximum(jnp.abs(den), jnp.exp(-m_t))[..., None]
        b_last = b[..., -1]
        log_w = b_last[..., None] - b + ig_
        m_new = jnp.maximum(b_last + m, jnp.max(log_w, -1))
        wk = jnp.exp(log_w - m_new[..., None])
        carry_decay = jnp.exp(b_last + m - m_new)
        C = carry_decay[..., None, None] * C + jnp.einsum("bhl,bhld,bhlv->bhdv", wk, k_, v_)
        n = carry_decay[..., None] * n + jnp.einsum("bhl,bhld->bhd", wk, k_)
        return (C, n, m_new), h

    init = (jnp.zeros((B, H, DK, DV), jnp.float32), jnp.zeros((B, H, DK), jnp.float32),
            jnp.zeros((B, H), jnp.float32))
    _, h = lax.scan(step, init, (qc, kc, vc, lfc, igc))
    return from_chunks(h)


def gated_delta_chunkwise(q, k, v, g, beta):
    B, S, H, DK = q.shape
    DV = v.shape[-1]
    L = G_CHUNK
    q = q * DK ** -0.5
    qc, kc, vc = to_chunks(q, L), to_chunks(k, L), to_chunks(v, L)
    bc = to_chunks(beta, L)
    G = jnp.cumsum(to_chunks(g, L), -1)
    incl = jnp.tril(jnp.ones((L, L), bool))
    strict = jnp.tril(jnp.ones((L, L), bool), -1)
    diff = G[..., :, None] - G[..., None, :]
    decay = jnp.where(incl, jnp.exp(jnp.where(incl, diff, 0.0)), 0.0)
    kk = jnp.einsum("nbhld,nbhsd->nbhls", kc, kc)
    a_mat = jnp.where(strict, bc[..., :, None] * kk * decay, 0.0) + jnp.eye(L, dtype=jnp.float32)
    rhs = jnp.concatenate([vc * bc[..., None], kc * (bc * jnp.exp(G))[..., None]], -1)
    sol = lax.linalg.triangular_solve(a_mat, rhs, left_side=True, lower=True, unit_diagonal=True)
    u, w = sol[..., :DV], sol[..., DV:]
    attn = jnp.einsum("nbhld,nbhsd->nbhls", qc, kc) * decay
    q_dec = qc * jnp.exp(G)[..., None]
    k_dec = kc * jnp.exp(G[..., -1:] - G)[..., None]
    g_last = jnp.exp(G[..., -1])

    def step(state, inp):
        u_, w_, attn_, qd_, kd_, gl_ = inp
        v_new = u_ - jnp.einsum("bhld,bhdv->bhlv", w_, state)
        o = jnp.einsum("bhld,bhdv->bhlv", qd_, state) + jnp.einsum("bhls,bhsv->bhlv", attn_, v_new)
        state = gl_[..., None, None] * state + jnp.einsum("bhld,bhlv->bhdv", kd_, v_new)
        return state, o

    _, o = lax.scan(step, jnp.zeros((B, H, DK, DV), jnp.float32), (u, w, attn, q_dec, k_dec, g_last))
    return from_chunks(o)


def hgrn2_chunkwise(q, k, v, log_f):
    B, S, H, DK = q.shape
    DV = v.shape[-1]
    L = C_CHUNK
    q = q * DK ** -0.5
    qc, kc, vc = to_chunks(q, L), to_chunks(k, L), to_chunks(v, L)
    G = jnp.cumsum(to_chunks(log_f, L), -2)
    q_dec = qc * jnp.exp(G)
    k_dec = kc * jnp.exp(G[..., -1:, :] - G)
    g_last = jnp.exp(G[..., -1, :])
    incl = jnp.tril(jnp.ones((L, L), bool))[:, :, None]

    def step(state, inp):
        q_, k_, v_, G_, qd_, kd_, gl_ = inp
        diff = G_[..., :, None, :] - G_[..., None, :, :]
        pair_decay = jnp.exp(jnp.where(incl, diff, -jnp.inf))
        attn = jnp.einsum("bhld,bhlsd,bhsd->bhls", q_, pair_decay, k_)
        o = jnp.einsum("bhld,bhdv->bhlv", qd_, state) + jnp.einsum("bhls,bhsv->bhlv", attn, v_)
        state = gl_[..., :, None] * state + jnp.einsum("bhld,bhlv->bhdv", kd_, v_)
        return state, o

    _, o = lax.scan(step, jnp.zeros((B, H, DK, DV), jnp.float32), (qc, kc, vc, G, q_dec, k_dec, g_last))
    return from_chunks(o)


def hgrn_lower_bounds(lb_logits):
    p = jax.nn.softmax(lb_logits.astype(jnp.float32), axis=0)
    return jnp.cumsum(p, axis=0) - p[0]


def ab_mixer(u, w_in, conv_m, m_gate_bias, m_norm, conv_g, g_a_log, g_dt_bias, g_norm, w_out):
    f32 = jnp.float32
    proj = u @ w_in
    m_q, m_k, m_v, m_o, m_i, m_f, g_q, g_k, g_v, g_gate, g_a, g_b = jnp.split(proj, AB_SPLIT_IDX, axis=-1)
    m_qk = jax.nn.silu(causal_dwconv(jnp.concatenate([m_q, m_k], -1), conv_m))
    m_q, m_k = jnp.split(m_qk, 2, axis=-1)
    gb = m_gate_bias.astype(f32)
    i_pre = softcap(m_i.astype(f32) + gb[0], GATE_SOFTCAP)
    f_pre = softcap(m_f.astype(f32) + gb[1], GATE_SOFTCAP)
    h_m = mlstm_chunkwise(to_heads(m_q, M_HEADS).astype(f32), to_heads(m_k, M_HEADS).astype(f32),
                          to_heads(m_v, M_HEADS).astype(f32), i_pre, f_pre)
    y_m = jax.nn.sigmoid(m_o.astype(f32)) * head_rms_norm(h_m, m_norm)
    g_qkv = jax.nn.silu(causal_dwconv(jnp.concatenate([g_q, g_k, g_v], -1), conv_g))
    g_q, g_k, g_v = jnp.split(g_qkv, 3, axis=-1)
    log_decay = -jnp.exp(g_a_log.astype(f32)) * jax.nn.softplus(g_a.astype(f32) + g_dt_bias.astype(f32))
    beta = jax.nn.sigmoid(g_b.astype(f32))
    h_g = gated_delta_chunkwise(l2_norm(to_heads(g_q, G_HEADS).astype(f32)),
                                l2_norm(to_heads(g_k, G_HEADS).astype(f32)),
                                to_heads(g_v, G_HEADS).astype(f32), log_decay, beta)
    y_g = head_rms_norm(h_g, g_norm) * jax.nn.silu(g_gate.astype(f32))
    y = jnp.concatenate([y_m, y_g], -1).astype(u.dtype)
    return y @ w_out


def hgrn2_mixer(u, w_in, lower_bound, norm_gain, w_out):
    f32 = jnp.float32
    q, f, i, gate = jnp.split(u @ w_in, [C_K_TOTAL, 2 * C_K_TOTAL, 2 * C_K_TOTAL + C_MIX], axis=-1)
    lb = lower_bound.astype(f32)
    ff = f.astype(f32)
    log_f = jnp.logaddexp(jnp.log(lb), jnp.log1p(-lb) + jax.nn.log_sigmoid(ff))
    k = (1.0 - lb) * jax.nn.sigmoid(-ff)
    o = hgrn2_chunkwise(to_heads(jax.nn.silu(q.astype(f32)), C_HEADS), to_heads(k, C_HEADS),
                        to_heads(i.astype(f32), C_HEADS), to_heads(log_f, C_HEADS))
    y = head_rms_norm(o, norm_gain) * jax.nn.silu(gate.astype(f32))
    return y.astype(u.dtype) @ w_out


def conv_ffn(u, w_up, conv_w, conv_b, w_down):
    hdn = causal_dwconv(u @ w_up, conv_w) + conv_b
    gate, val = jnp.split(hdn, 2, axis=-1)
    return (jax.nn.silu(gate) * val) @ w_down


def setup_inputs(seed: int = 0) -> dict:
    key = jax.random.key(seed)
    ks = jax.random.split(key, 24)
    f32 = jnp.float32

    def nrm(k, shape, scale):
        return jax.random.normal(k, shape, f32) * scale

    dt = jnp.exp(jax.random.uniform(ks[9], (N_AB_LAYERS, G_HEADS), f32, np.log(1e-3), np.log(1e-1)))
    return {
        "x": nrm(ks[0], (BATCH, SEQ, D_MODEL), 1.0),
        "norm_gains": 1.0 + nrm(ks[1], (DEPTH, 4, D_MODEL), 0.05),
        "ab_w_in": nrm(ks[2], (N_AB_LAYERS, D_MODEL, AB_IN), D_MODEL ** -0.5),
        "ab_conv_m": nrm(ks[3], (N_AB_LAYERS, SHORT_CONV, 2 * M_QK), SHORT_CONV ** -0.5),
        "ab_m_gate_bias": jnp.stack([nrm(ks[4], (N_AB_LAYERS, M_HEADS), 0.1),
                                     jnp.linspace(3.0, 6.0, M_HEADS, dtype=f32) + nrm(ks[5], (N_AB_LAYERS, M_HEADS), 0.1)], axis=1),
        "ab_m_norm": 1.0 + nrm(ks[6], (N_AB_LAYERS, M_V), 0.05),
        "ab_conv_g": nrm(ks[7], (N_AB_LAYERS, SHORT_CONV, 2 * G_QK + G_V), SHORT_CONV ** -0.5),
        "ab_g_a_log": jnp.log(jax.random.uniform(ks[8], (N_AB_LAYERS, G_HEADS), f32, 1.0, 16.0)),
        "ab_g_dt_bias": dt + jnp.log(-jnp.expm1(-dt)),
        "ab_g_norm": 1.0 + nrm(ks[10], (N_AB_LAYERS, G_V), 0.05),
        "ab_w_out": nrm(ks[11], (N_AB_LAYERS, AB_MIX, D_MODEL), AB_MIX ** -0.5),
        "c_w_in": nrm(ks[12], (N_C_LAYERS, D_MODEL, C_IN), D_MODEL ** -0.5),
        "c_lb_logits": nrm(ks[13], (DEPTH, C_K_TOTAL), 0.5),
        "c_norm": 1.0 + nrm(ks[14], (N_C_LAYERS, C_MIX), 0.05),
        "c_w_out": nrm(ks[15], (N_C_LAYERS, C_MIX, D_MODEL), C_MIX ** -0.5),
        "ffn_w_up": nrm(ks[16], (DEPTH, D_MODEL, 2 * D_FF), D_MODEL ** -0.5),
        "ffn_conv_w": nrm(ks[17], (DEPTH, FFN_CONV, 2 * D_FF), FFN_CONV ** -0.5),
        "ffn_conv_b": nrm(ks[18], (DEPTH, 2 * D_FF), 0.02),
        "ffn_w_down": nrm(ks[19], (DEPTH, D_FF, D_MODEL), D_FF ** -0.5),
    }


def reference(x, norm_gains, ab_w_in, ab_conv_m, ab_m_gate_bias, ab_m_norm, ab_conv_g, ab_g_a_log,
              ab_g_dt_bias, ab_g_norm, ab_w_out, c_w_in, c_lb_logits, c_norm, c_w_out,
              ffn_w_up, ffn_conv_w, ffn_conv_b, ffn_w_down):
    lower_bounds = hgrn_lower_bounds(c_lb_logits)
    for layer in range(DEPTH):
        j = layer // 2
        u = rms_norm(x, norm_gains[layer, 0])
        if layer % 2 == 0:
            mix = ab_mixer(u, ab_w_in[j], ab_conv_m[j], ab_m_gate_bias[j], ab_m_norm[j], ab_conv_g[j],
                           ab_g_a_log[j], ab_g_dt_bias[j], ab_g_norm[j], ab_w_out[j])
        else:
            mix = hgrn2_mixer(u, c_w_in[j], lower_bounds[layer], c_norm[j], c_w_out[j])
        x = x + rms_norm(mix, norm_gains[layer, 1])
        u = rms_norm(x, norm_gains[layer, 2])
        ff = conv_ffn(u, ffn_w_up[layer], ffn_conv_w[layer], ffn_conv_b[layer], ffn_w_down[layer])
        x = x + rms_norm(ff, norm_gains[layer, 3])
    return x
```

```python
import functools

import jax
import jax.numpy as jnp
from jax import lax
from jax.experimental import pallas as pl
from jax.experimental.pallas import tpu as pltpu

F32 = jnp.float32
BF16 = jnp.bfloat16
EPS = 1e-6
HALO = 8
VMEM_LIMIT_BYTES = 56 * 1024 * 1024


def _rms(x, gain):
    return x * lax.rsqrt(jnp.mean(x * x, axis=-1, keepdims=True) + EPS) * gain


def _sigmoid(x):
    return 1.0 / (1.0 + jnp.exp(-x))


def _silu(x):
    return x * _sigmoid(x)


def _resident(shape):
    nd = len(shape)
    return pl.BlockSpec(shape, lambda b, s: (0,) * nd, pipeline_mode=pl.Buffered(1))


def _ffn_kernel(x_ref, g_pre_ref, g_post_ref, wup_ref, cw_ref, cb_ref, wdn_ref, o_ref,
                u_scr, a_scr, h_scr, carry_scr, *, d_ff, slab):
    tb = x_ref.shape[1]

    @pl.when(pl.program_id(1) == 0)
    def _():
        carry_scr[...] = jnp.zeros_like(carry_scr)

    u_scr[...] = _rms(x_ref[0], g_pre_ref[...]).astype(BF16)

    def conv_half(c0):
        h = jnp.dot(u_scr[...], wup_ref[:, c0:c0 + slab], preferred_element_type=F32)
        h_scr[0:HALO, :] = carry_scr[:, c0:c0 + slab]
        h_scr[HALO:HALO + tb, :] = h
        carry_scr[:, c0:c0 + slab] = h[tb - HALO:, :]
        w = cw_ref[:, c0:c0 + slab]
        return (w[0:1] * h_scr[HALO - 2:HALO - 2 + tb, :] + w[1:2] * h_scr[HALO - 1:HALO - 1 + tb, :]
                + w[2:3] * h + cb_ref[:, c0:c0 + slab])

    for j in range(d_ff // slab):
        gate = conv_half(j * slab)
        val = conv_half(d_ff + j * slab)
        a_scr[:, j * slab:(j + 1) * slab] = (_silu(gate) * val).astype(BF16)

    out = jnp.dot(a_scr[...], wdn_ref[...], preferred_element_type=F32)
    o_ref[0] = x_ref[0] + _rms(out, g_post_ref[...])


def _ffn(x, g_pre, g_post, w_up, conv_w, conv_b, w_down, *, block=512, slab=256):
    B, S, D = x.shape
    d_ff = w_down.shape[0]
    tb = min(block, S)
    assert S % tb == 0 and d_ff % slab == 0
    kern = functools.partial(_ffn_kernel, d_ff=d_ff, slab=slab)
    return pl.pallas_call(
        kern,
        grid=(B, S // tb),
        in_specs=[
            pl.BlockSpec((1, tb, D), lambda b, s: (b, s, 0)),
            _resident((1, D)), _resident((1, D)),
            _resident((D, 2 * d_ff)), _resident((3, 2 * d_ff)), _resident((1, 2 * d_ff)),
            _resident((d_ff, D)),
        ],
        out_specs=pl.BlockSpec((1, tb, D), lambda b, s: (b, s, 0)),
        out_shape=jax.ShapeDtypeStruct((B, S, D), F32),
        scratch_shapes=[
            pltpu.VMEM((tb, D), BF16),
            pltpu.VMEM((tb, d_ff), BF16),
            pltpu.VMEM((HALO + tb, slab), F32),
            pltpu.VMEM((HALO, 2 * d_ff), F32),
        ],
        compiler_params=pltpu.CompilerParams(
            dimension_semantics=("arbitrary", "arbitrary"), vmem_limit_bytes=VMEM_LIMIT_BYTES),
        name="ffn",
    )(x, g_pre.reshape(1, D), g_post.reshape(1, D), w_up.astype(BF16), conv_w,
      conv_b.reshape(1, 2 * d_ff), w_down.astype(BF16))


def _dot(a, b):
    return jnp.dot(a, b, preferred_element_type=F32)


def _dot_nt(a, b):
    return lax.dot_general(a, b, (((1,), (1,)), ((), ())), preferred_element_type=F32)


def _dot_tn(a, b):
    return lax.dot_general(a, b, (((0,), (0,)), ((), ())), preferred_element_type=F32)


def _split3(x):
    hi = x.astype(BF16)
    r1 = x - hi.astype(F32)
    mid = r1.astype(BF16)
    lo = (r1 - mid.astype(F32)).astype(BF16)
    return jnp.concatenate([hi, mid, lo], axis=0)


def _log_sigmoid(x):
    return jnp.minimum(x, 0.0) - jnp.log1p(jnp.exp(-jnp.abs(x)))


def _level_sizes(chunk):
    sizes, m = [], chunk // 2
    while m >= 1:
        sizes.append(m)
        m //= 2
    return sizes


def _hgrn2_tables(chunk):
    import numpy as np
    L = chunk
    idx = np.arange(L)
    mats = [(idx[None, :] <= idx[:, None]).astype(np.float32)]
    masks = []
    for m in _level_sizes(L):
        in_b = (idx % (2 * m)) >= m
        blk_start = (idx // m) * m
        blk_end = blk_start + m - 1
        c = idx[None, :]
        mat_b = (c >= blk_start[:, None]) & (c <= idx[:, None])
        mat_a = (c > idx[:, None]) & (c <= blk_end[:, None])
        mats.append(np.where(in_b[:, None], mat_b, mat_a).astype(np.float32))
        same_pair = (idx[:, None] // (2 * m)) == (idx[None, :] // (2 * m))
        masks.append((same_pair & in_b[:, None] & (~in_b)[None, :]).astype(np.float32))
    masks.append(np.eye(L, dtype=np.float32))
    mstack = np.concatenate(mats, axis=0)
    return np.concatenate([mstack] * 3, axis=1), np.stack(masks)


def _hgrn2_kernel(x_ref, g_pre_ref, g_post_ref, win_ref, lb_ref, ng_ref, wout_ref, mstack_ref, mask_ref,
                  o_ref, u_scr, proj_scr, y_scr, q_scr, k_scr, xall_scr, st_scr, *, heads, dk, chunk):
    tb = x_ref.shape[1]
    hk = heads * dk
    n_lvl = len(_level_sizes(chunk))

    @pl.when(pl.program_id(1) == 0)
    def _():
        st_scr[...] = jnp.zeros_like(st_scr)

    u_scr[...] = _rms(x_ref[0], g_pre_ref[...]).astype(BF16)
    for j in range(4):
        proj_scr[:, j * hk:(j + 1) * hk] = _dot(u_scr[...], win_ref[:, j * hk:(j + 1) * hk])

    lb = lb_ref[...]
    log_lb = jnp.log(lb)
    log1m_lb = jnp.log1p(-lb)

    def chunk_body(c, carry):
        r0 = pl.multiple_of(c * chunk, chunk)
        rows = pl.ds(r0, chunk)
        ff = proj_scr[rows, hk:2 * hk]
        q_scr[...] = _silu(proj_scr[rows, 0:hk]) * (dk ** -0.5)
        k_scr[...] = (1.0 - lb) * _sigmoid(-ff)
        a = log1m_lb + _log_sigmoid(ff)
        log_f = jnp.maximum(log_lb, a) + jnp.log1p(jnp.exp(-jnp.abs(log_lb - a)))
        xall_scr[...] = _dot(mstack_ref[...], _split3(log_f))

        for h in range(heads):
            hc = slice(h * dk, (h + 1) * dk)
            q = q_scr[:, hc]
            k = k_scr[:, hc]
            v = proj_scr[rows, 2 * hk + h * dk:2 * hk + (h + 1) * dk].astype(BF16)
            g = xall_scr[0:chunk, hc]
            g_last = g[chunk - 1:chunk, :]
            attn = _dot_nt(q.astype(BF16), k.astype(BF16)) * mask_ref[n_lvl]
            for lvl in range(n_lvl):
                e = jnp.exp(xall_scr[(lvl + 1) * chunk:(lvl + 2) * chunk, hc])
                attn = attn + _dot_nt((q * e).astype(BF16), (k * e).astype(BF16)) * mask_ref[lvl]
            st = st_scr[h]
            q_dec = (q * jnp.exp(g)).astype(BF16)
            k_dec = (k * jnp.exp(g_last - g)).astype(BF16)
            o = _dot_nt(q_dec, st.astype(BF16)) + _dot(attn.astype(BF16), v)
            st_scr[h] = st * jnp.exp(g_last) + _dot_tn(v, k_dec)
            gate = proj_scr[rows, 3 * hk + h * dk:3 * hk + (h + 1) * dk]
            y = o * lax.rsqrt(jnp.mean(o * o, axis=-1, keepdims=True) + EPS) * ng_ref[:, hc] * _silu(gate)
            y_scr[rows, hc] = y.astype(BF16)
        return carry

    lax.fori_loop(0, tb // chunk, chunk_body, 0)
    out = _dot(y_scr[...], wout_ref[...])
    o_ref[0] = x_ref[0] + _rms(out, g_post_ref[...])


def _hgrn2_mixer(x, g_pre, g_post, w_in, lower_bound, norm_gain, w_out, *, heads, block=512, chunk=64):
    B, S, D = x.shape
    hk = w_out.shape[0]
    dk = hk // heads
    tb = min(block, S)
    assert S % tb == 0 and tb % chunk == 0 and w_in.shape[1] == 4 * hk
    mstack, masks = _hgrn2_tables(chunk)
    n_rows = mstack.shape[0]
    kern = functools.partial(_hgrn2_kernel, heads=heads, dk=dk, chunk=chunk)
    return pl.pallas_call(
        kern,
        grid=(B, S // tb),
        in_specs=[
            pl.BlockSpec((1, tb, D), lambda b, s: (b, s, 0)),
            _resident((1, D)), _resident((1, D)),
            _resident((D, 4 * hk)), _resident((1, hk)), _resident((1, hk)), _resident((hk, D)),
            _resident(mstack.shape), _resident(masks.shape),
        ],
        out_specs=pl.BlockSpec((1, tb, D), lambda b, s: (b, s, 0)),
        out_shape=jax.ShapeDtypeStruct((B, S, D), F32),
        scratch_shapes=[
            pltpu.VMEM((tb, D), BF16),
            pltpu.VMEM((tb, 4 * hk), F32),
            pltpu.VMEM((tb, hk), BF16),
            pltpu.VMEM((chunk, hk), F32),
            pltpu.VMEM((chunk, hk), F32),
            pltpu.VMEM((n_rows, hk), F32),
            pltpu.VMEM((heads, dk, dk), F32),
        ],
        compiler_params=pltpu.CompilerParams(
            dimension_semantics=("arbitrary", "arbitrary"), vmem_limit_bytes=VMEM_LIMIT_BYTES),
        name="hgrn2_mixer",
    )(x, g_pre.reshape(1, D), g_post.reshape(1, D), w_in.astype(BF16), lower_bound.reshape(1, hk),
      norm_gain.reshape(1, hk), w_out.astype(BF16), jnp.asarray(mstack, BF16), jnp.asarray(masks, F32))


M_HEADS, M_DK, M_DV = 4, 64, 128
G_HEADS, G_DK, G_DV = 4, 128, 128
M_QK = M_HEADS * M_DK
M_V = M_HEADS * M_DV
G_QK = G_HEADS * G_DK
G_V = G_HEADS * G_DV
AB_CHUNK = 64
CONV_K = 4
GATE_SOFTCAP = 15.0
GATE_LANES = 128
OFF_MQ, OFF_MK = 0, M_QK
OFF_MV = 2 * M_QK
OFF_MO = OFF_MV + M_V
OFF_GQ = OFF_MO + M_V
OFF_GK = OFF_GQ + G_QK
OFF_GV = OFF_GK + G_QK
OFF_GG = OFF_GV + G_V
OFF_GATES = OFF_GG + G_V
AB_COLS = OFF_GATES + GATE_LANES
LANE_I, LANE_F, LANE_A, LANE_B = 0, M_HEADS, 2 * M_HEADS, 2 * M_HEADS + G_HEADS
ACT_MQ, ACT_MK = 0, M_QK
ACT_GQ = 2 * M_QK
ACT_GK = ACT_GQ + G_QK
ACT_GV = ACT_GK + G_QK
ACT_COLS = ACT_GV + G_V


def _ab_tables(chunk):
    import numpy as np
    idx = np.arange(chunk)
    l, s = idx[:, None], idx[None, :]
    incl = l >= s
    strict = l > s
    blk16 = (l // 16) == (s // 16)
    blk32 = (l // 32) == (s // 32)
    masks = np.stack([incl, strict, blk16, blk32 & ~blk16, ~blk32, l == s]).astype(np.float32)
    tri3 = np.concatenate([incl.astype(np.float32)] * 3, axis=1)
    return tri3, masks


def _col(a, lane_ids, j):
    return jnp.sum(jnp.where(lane_ids == j, a, 0.0), axis=-1, keepdims=True)


def _unit_lower_inverse(n, blk16, off32, off64, eye):
    d = (n * blk16).astype(BF16)
    d2 = _dot(d, d)
    d4 = _dot(d2.astype(BF16), d2.astype(BF16))
    d8 = _dot(d4.astype(BF16), d4.astype(BF16))
    p = _dot((eye - d).astype(BF16), (eye + d2).astype(BF16))
    p = _dot(p.astype(BF16), (eye + d4).astype(BF16))
    p = _dot(p.astype(BF16), (eye + d8).astype(BF16))
    for off in (off32, off64):
        pb = p.astype(BF16)
        p = p - _dot(_dot(pb, (n * off).astype(BF16)).astype(BF16), pb)
    return p


def _ab_kernel(x_ref, g_pre_ref, g_post_ref, win_ref, cm_ref, cg_ref, gbias_ref, alog_ref, mn_ref, gn_ref,
               wout_ref, tri3_ref, mask_ref, o_ref,
               u_scr, proj_scr, act_scr, y_scr, cn_scr, m_scr, s_scr, *, chunk):
    tb = x_ref.shape[1]

    @pl.when(pl.program_id(1) == 0)
    def _():
        proj_scr[0:HALO, :] = jnp.zeros((HALO, AB_COLS), F32)
        cn_scr[...] = jnp.zeros_like(cn_scr)
        m_scr[...] = jnp.zeros_like(m_scr)
        s_scr[...] = jnp.zeros_like(s_scr)

    u_scr[...] = _rms(x_ref[0], g_pre_ref[...]).astype(BF16)
    for c0 in range(0, AB_COLS, 512):
        c1 = min(c0 + 512, AB_COLS)
        proj_scr[HALO:HALO + tb, c0:c1] = _dot(u_scr[...], win_ref[:, c0:c1])

    def conv_silu(r0, col0, width, w_ref, wcol0):
        acc = None
        for j in range(CONV_K):
            xs = proj_scr[r0 - (CONV_K - 1) + j:r0 - (CONV_K - 1) + j + chunk, col0:col0 + width]
            t = w_ref[j:j + 1, wcol0:wcol0 + width] * xs
            acc = t if acc is None else acc + t
        return _silu(acc)

    def unit(v):
        return v * lax.rsqrt(jnp.sum(v * v, axis=-1, keepdims=True) + EPS)

    for ci in range(tb // chunk):
        r0 = HALO + ci * chunk
        a0 = ci * chunk
        mq = conv_silu(r0, OFF_MQ, M_QK, cm_ref, 0)
        act_scr[a0:a0 + chunk, ACT_MQ:ACT_MQ + M_QK] = mq * (M_DK ** -0.5)
        act_scr[a0:a0 + chunk, ACT_MK:ACT_MK + M_QK] = conv_silu(r0, OFF_MK, M_QK, cm_ref, M_QK)
        for h in range(G_HEADS):
            gq = conv_silu(r0, OFF_GQ + h * G_DK, G_DK, cg_ref, h * G_DK)
            act_scr[a0:a0 + chunk, ACT_GQ + h * G_DK:ACT_GQ + (h + 1) * G_DK] = unit(gq) * (G_DK ** -0.5)
            gk = conv_silu(r0, OFF_GK + h * G_DK, G_DK, cg_ref, G_QK + h * G_DK)
            act_scr[a0:a0 + chunk, ACT_GK + h * G_DK:ACT_GK + (h + 1) * G_DK] = unit(gk)
        act_scr[a0:a0 + chunk, ACT_GV:ACT_GV + G_V] = conv_silu(r0, OFF_GV, G_V, cg_ref, 2 * G_QK)
    proj_scr[0:HALO, :] = proj_scr[tb:tb + HALO, :]

    lane = lax.broadcasted_iota(jnp.int32, (chunk, GATE_LANES), 1)
    gbias = gbias_ref[...]
    neg_a = -jnp.exp(alog_ref[...])
    ones_lane0 = jnp.where(lax.broadcasted_iota(jnp.int32, (chunk, M_DV), 1) == 0, 1.0, 0.0).astype(BF16)

    def chunk_body(c, carry):
        a0 = pl.multiple_of(c * chunk, chunk)
        arow = pl.ds(a0, chunk)
        prow = pl.ds(a0 + HALO, chunk)
        m_incl, m_strict, m_blk16, m_off32, m_off64, m_eye = (mask_ref[i] for i in range(6))

        z = proj_scr[prow, OFF_GATES:OFF_GATES + GATE_LANES] + gbias
        capped = GATE_SOFTCAP * jnp.tanh(z / GATE_SOFTCAP)
        softplus = jnp.maximum(z, 0.0) + jnp.log1p(jnp.exp(-jnp.abs(z)))
        gp = jnp.where(lane < LANE_F, capped,
                       jnp.where(lane < LANE_A, _log_sigmoid(capped),
                                 jnp.where(lane < LANE_B, neg_a * softplus, _sigmoid(z))))
        cs = _dot(tri3_ref[...], _split3(gp))
        gp_t = gp.T
        cs_t = cs.T

        for h in range(M_HEADS):
            q = act_scr[arow, ACT_MQ + h * M_DK:ACT_MQ + (h + 1) * M_DK].astype(BF16)
            k = act_scr[arow, ACT_MK + h * M_DK:ACT_MK + (h + 1) * M_DK]
            v = proj_scr[prow, OFF_MV + h * M_DV:OFF_MV + (h + 1) * M_DV].astype(BF16)
            v_ext = jnp.concatenate([v, ones_lane0], axis=1)
            b_col = _col(cs, lane, LANE_F + h)
            ig_col = _col(gp, lane, LANE_I + h)
            b_row = cs_t[LANE_F + h:LANE_F + h + 1, :]
            ig_row = gp_t[LANE_I + h:LANE_I + h + 1, :]
            m_prev = m_scr[h]
            log_d = jnp.where(m_incl > 0.0, b_col - b_row + ig_row, -jnp.inf)
            log_inter = b_col + m_prev
            m_t = jnp.maximum(jnp.max(log_d, axis=-1, keepdims=True), log_inter)
            d = jnp.exp(log_d - m_t)
            w_inter = jnp.exp(log_inter - m_t)
            s = _dot_nt(q, k.astype(BF16)) * d
            cn = cn_scr[h]
            nd = _dot(s.astype(BF16), v_ext) + w_inter * _dot(q, cn.astype(BF16))
            num = nd[:, 0:M_DV]
            den = nd[:, M_DV:M_DV + 1]
            hm = num / jnp.maximum(jnp.abs(den), jnp.exp(-m_t))
            b_last = b_col[chunk - 1:chunk, :]
            log_w = b_last - b_col + ig_col
            m_new = jnp.maximum(b_last + m_prev, jnp.max(log_w, axis=0, keepdims=True))
            wk = jnp.exp(log_w - m_new)
            cn_scr[h] = jnp.exp(b_last + m_prev - m_new) * cn + _dot_tn((k * wk).astype(BF16), v_ext)
            m_scr[h] = m_new
            gate = proj_scr[prow, OFF_MO + h * M_DV:OFF_MO + (h + 1) * M_DV]
            hn = hm * lax.rsqrt(jnp.mean(hm * hm, axis=-1, keepdims=True) + EPS)
            y_scr[arow, h * M_DV:(h + 1) * M_DV] = (
                _sigmoid(gate) * hn * mn_ref[:, h * M_DV:(h + 1) * M_DV]).astype(BF16)

        for h in range(G_HEADS):
            q = act_scr[arow, ACT_GQ + h * G_DK:ACT_GQ + (h + 1) * G_DK]
            k = act_scr[arow, ACT_GK + h * G_DK:ACT_GK + (h + 1) * G_DK]
            v = act_scr[arow, ACT_GV + h * G_DV:ACT_GV + (h + 1) * G_DV]
            g_col = _col(cs, lane, LANE_A + h)
            beta = _col(gp, lane, LANE_B + h)
            g_row = cs_t[LANE_A + h:LANE_A + h + 1, :]
            decay = jnp.where(m_incl > 0.0, jnp.exp(jnp.where(m_incl > 0.0, g_col - g_row, 0.0)), 0.0)
            kb = k.astype(BF16)
            n = m_strict * (beta * _dot_nt(kb, kb) * decay)
            t_inv = _unit_lower_inverse(n, m_blk16, m_off32, m_off64, m_eye)
            exp_g = jnp.exp(g_col)
            rhs = jnp.concatenate([v * beta, k * (beta * exp_g)], axis=1)
            sol = _dot(t_inv.astype(BF16), rhs.astype(BF16))
            u_, w_ = sol[:, 0:G_DV], sol[:, G_DV:G_DV + G_DK]
            attn = _dot_nt(q.astype(BF16), kb) * decay
            g_last = g_col[chunk - 1:chunk, :]
            q_dec = q * exp_g
            k_dec = k * jnp.exp(g_last - g_col)
            st = s_scr[h]
            wq = _dot(jnp.concatenate([w_, q_dec], axis=0).astype(BF16), st.astype(BF16))
            v_new = u_ - wq[0:chunk]
            o = wq[chunk:2 * chunk] + _dot(attn.astype(BF16), v_new.astype(BF16))
            s_scr[h] = jnp.exp(g_last) * st + _dot_tn(k_dec.astype(BF16), v_new.astype(BF16))
            gate = proj_scr[prow, OFF_GG + h * G_DV:OFF_GG + (h + 1) * G_DV]
            on = o * lax.rsqrt(jnp.mean(o * o, axis=-1, keepdims=True) + EPS)
            y_scr[arow, M_V + h * G_DV:M_V + (h + 1) * G_DV] = (
                on * gn_ref[:, h * G_DV:(h + 1) * G_DV] * _silu(gate)).astype(BF16)
        return carry

    lax.fori_loop(0, tb // chunk, chunk_body, 0)
    out = _dot(y_scr[...], wout_ref[...])
    o_ref[0] = x_ref[0] + _rms(out, g_post_ref[...])


def _ab_pack_w_in(w_in):
    D = w_in.shape[0]
    sizes = (M_QK, M_QK, M_V, M_V, M_HEADS, M_HEADS, G_QK, G_QK, G_V, G_V, G_HEADS, G_HEADS)
    offs = [0]
    for sz in sizes:
        offs.append(offs[-1] + sz)
    seg = [w_in[:, offs[i]:offs[i + 1]] for i in range(len(sizes))]
    m_q, m_k, m_v, m_o, m_i, m_f, g_q, g_k, g_v, g_gate, g_a, g_b = seg
    gates = jnp.concatenate([m_i, m_f, g_a, g_b], axis=1)
    gates = jnp.pad(gates, ((0, 0), (0, GATE_LANES - gates.shape[1])))
    return jnp.concatenate([m_q, m_k, m_v, m_o, g_q, g_k, g_v, g_gate, gates], axis=1).astype(BF16)


def _lane_vec(pieces):
    v = jnp.zeros((GATE_LANES,), F32)
    for off, vals in pieces:
        v = lax.dynamic_update_slice(v, vals.astype(F32), (off,))
    return v.reshape(1, GATE_LANES)


def _ab_mixer(x, g_pre, g_post, w_in, conv_m, m_gate_bias, m_norm, conv_g, g_a_log, g_dt_bias, g_norm, w_out,
              *, block=512):
    B, S, D = x.shape
    chunk = AB_CHUNK
    tb = min(block, S)
    assert S % tb == 0 and tb % chunk == 0
    tri3, masks = _ab_tables(chunk)
    gbias = _lane_vec([(LANE_I, m_gate_bias[0]), (LANE_F, m_gate_bias[1]), (LANE_A, g_dt_bias)])
    alog = _lane_vec([(LANE_A, g_a_log)])
    kern = functools.partial(_ab_kernel, chunk=chunk)
    mix = M_V + G_V
    return pl.pallas_call(
        kern,
        grid=(B, S // tb),
        in_specs=[
            pl.BlockSpec((1, tb, D), lambda b, s: (b, s, 0)),
            _resident((1, D)), _resident((1, D)),
            _resident((D, AB_COLS)), _resident(conv_m.shape), _resident(conv_g.shape),
            _resident((1, GATE_LANES)), _resident((1, GATE_LANES)),
            _resident((1, M_V)), _resident((1, G_V)), _resident((mix, D)),
            _resident(tri3.shape), _resident(masks.shape),
        ],
        out_specs=pl.BlockSpec((1, tb, D), lambda b, s: (b, s, 0)),
        out_shape=jax.ShapeDtypeStruct((B, S, D), F32),
        scratch_shapes=[
            pltpu.VMEM((tb, D), BF16),
            pltpu.VMEM((HALO + tb, AB_COLS), F32),
            pltpu.VMEM((tb, ACT_COLS), F32),
            pltpu.VMEM((tb, mix), BF16),
            pltpu.VMEM((M_HEADS, M_DK, M_DV + 128), F32),
            pltpu.VMEM((M_HEADS, 1, 1), F32),
            pltpu.VMEM((G_HEADS, G_DK, G_DV), F32),
        ],
        compiler_params=pltpu.CompilerParams(
            dimension_semantics=("arbitrary", "arbitrary"), vmem_limit_bytes=VMEM_LIMIT_BYTES),
        name="ab_mixer",
    )(x, g_pre.reshape(1, D), g_post.reshape(1, D), _ab_pack_w_in(w_in), conv_m, conv_g, gbias, alog,
      m_norm.reshape(1, M_V), g_norm.reshape(1, G_V), w_out.astype(BF16),
      jnp.asarray(tri3, BF16), jnp.asarray(masks, F32))


def kernel(x, norm_gains, ab_w_in, ab_conv_m, ab_m_gate_bias, ab_m_norm, ab_conv_g, ab_g_a_log, ab_g_dt_bias, ab_g_norm, ab_w_out, c_w_in, c_lb_logits, c_norm, c_w_out, ffn_w_up, ffn_conv_w, ffn_conv_b, ffn_w_down):
    depth = norm_gains.shape[0]
    p = jax.nn.softmax(c_lb_logits.astype(F32), axis=0)
    lower_bounds = jnp.cumsum(p, axis=0) - p[0]
    c_heads = c_w_out.shape[1] // 128
    for layer in range(depth):
        j = layer // 2
        g = norm_gains[layer]
        if layer % 2 == 0:
            x = _ab_mixer(x, g[0], g[1], ab_w_in[j], ab_conv_m[j], ab_m_gate_bias[j], ab_m_norm[j],
                          ab_conv_g[j], ab_g_a_log[j], ab_g_dt_bias[j], ab_g_norm[j], ab_w_out[j])
        else:
            x = _hgrn2_mixer(x, g[0], g[1], c_w_in[j], lower_bounds[layer], c_norm[j], c_w_out[j],
                             heads=c_heads)
        x = _ffn(x, g[2], g[3], ffn_w_up[layer], ffn_conv_w[layer], ffn_conv_b[layer], ffn_w_down[layer])
    return x
```

```python
import functools

import jax
import jax.numpy as jnp
from jax import lax
from jax.experimental import pallas as pl
from jax.experimental.pallas import tpu as pltpu

F32 = jnp.float32
BF16 = jnp.bfloat16
EPS = 1e-6
HALO = 8
VMEM_LIMIT_BYTES = 56 * 1024 * 1024


def _rms(x, gain):
    return x * lax.rsqrt(jnp.mean(x * x, axis=-1, keepdims=True) + EPS) * gain


def _sigmoid(x):
    return 1.0 / (1.0 + jnp.exp(-x))


def _silu(x):
    return x * _sigmoid(x)


def _resident(shape):
    nd = len(shape)
    return pl.BlockSpec(shape, lambda b, s: (0,) * nd, pipeline_mode=pl.Buffered(1))


def _ffn_kernel(x_ref, g_pre_ref, g_post_ref, wup_ref, cw_ref, cb_ref, wdn_ref, o_ref,
                u_scr, a_scr, h_scr, carry_scr, *, d_ff, slab):
    tb = x_ref.shape[1]

    @pl.when(pl.program_id(1) == 0)
    def _():
        carry_scr[...] = jnp.zeros_like(carry_scr)

    u_scr[...] = _rms(x_ref[0], g_pre_ref[...]).astype(BF16)

    def conv_half(c0):
        h = jnp.dot(u_scr[...], wup_ref[:, c0:c0 + slab], preferred_element_type=F32)
        h_scr[0:HALO, :] = carry_scr[:, c0:c0 + slab]
        h_scr[HALO:HALO + tb, :] = h
        carry_scr[:, c0:c0 + slab] = h[tb - HALO:, :]
        w = cw_ref[:, c0:c0 + slab]
        return (w[0:1] * h_scr[HALO - 2:HALO - 2 + tb, :] + w[1:2] * h_scr[HALO - 1:HALO - 1 + tb, :]
                + w[2:3] * h + cb_ref[:, c0:c0 + slab])

    for j in range(d_ff // slab):
        gate = conv_half(j * slab)
        val = conv_half(d_ff + j * slab)
        a_scr[:, j * slab:(j + 1) * slab] = (_silu(gate) * val).astype(BF16)

    out = jnp.dot(a_scr[...], wdn_ref[...], preferred_element_type=F32)
    o_ref[0] = x_ref[0] + _rms(out, g_post_ref[...])


def _ffn(x, g_pre, g_post, w_up, conv_w, conv_b, w_down, *, block=512, slab=256):
    B, S, D = x.shape
    d_ff = w_down.shape[0]
    tb = min(block, S)
    assert S % tb == 0 and d_ff % slab == 0
    kern = functools.partial(_ffn_kernel, d_ff=d_ff, slab=slab)
    return pl.pallas_call(
        kern,
        grid=(B, S // tb),
        in_specs=[
            pl.BlockSpec((1, tb, D), lambda b, s: (b, s, 0)),
            _resident((1, D)), _resident((1, D)),
            _resident((D, 2 * d_ff)), _resident((3, 2 * d_ff)), _resident((1, 2 * d_ff)),
            _resident((d_ff, D)),
        ],
        out_specs=pl.BlockSpec((1, tb, D), lambda b, s: (b, s, 0)),
        out_shape=jax.ShapeDtypeStruct((B, S, D), F32),
        scratch_shapes=[
            pltpu.VMEM((tb, D), BF16),
            pltpu.VMEM((tb, d_ff), BF16),
            pltpu.VMEM((HALO + tb, slab), F32),
            pltpu.VMEM((HALO, 2 * d_ff), F32),
        ],
        compiler_params=pltpu.CompilerParams(
            dimension_semantics=("arbitrary", "arbitrary"), vmem_limit_bytes=VMEM_LIMIT_BYTES),
        name="ffn",
    )(x, g_pre.reshape(1, D), g_post.reshape(1, D), w_up.astype(BF16), conv_w,
      conv_b.reshape(1, 2 * d_ff), w_down.astype(BF16))


def _dot(a, b):
    return jnp.dot(a, b, preferred_element_type=F32)


def _dot_nt(a, b):
    return lax.dot_general(a, b, (((1,), (1,)), ((), ())), preferred_element_type=F32)


def _dot_tn(a, b):
    return lax.dot_general(a, b, (((0,), (0,)), ((), ())), preferred_element_type=F32)


def _bf(x):
    return x.astype(BF16)


def _split3(x):
    hi = x.astype(BF16)
    r1 = x - hi.astype(F32)
    mid = r1.astype(BF16)
    lo = (r1 - mid.astype(F32)).astype(BF16)
    return jnp.concatenate([hi, mid, lo], axis=0)


def _log_sigmoid(x):
    return jnp.minimum(x, 0.0) - jnp.log1p(jnp.exp(-jnp.abs(x)))


def _level_sizes(chunk):
    sizes, m = [], chunk // 2
    while m >= 1:
        sizes.append(m)
        m //= 2
    return sizes


def _hgrn2_tables(chunk):
    import numpy as np
    L = chunk
    idx = np.arange(L)
    mats = [(idx[None, :] <= idx[:, None]).astype(np.float32)]
    masks = []
    for m in _level_sizes(L):
        in_b = (idx % (2 * m)) >= m
        blk_start = (idx // m) * m
        blk_end = blk_start + m - 1
        c = idx[None, :]
        mat_b = (c >= blk_start[:, None]) & (c <= idx[:, None])
        mat_a = (c > idx[:, None]) & (c <= blk_end[:, None])
        mats.append(np.where(in_b[:, None], mat_b, mat_a).astype(np.float32))
        same_pair = (idx[:, None] // (2 * m)) == (idx[None, :] // (2 * m))
        masks.append((same_pair & in_b[:, None] & (~in_b)[None, :]).astype(np.float32))
    masks.append(np.eye(L, dtype=np.float32))
    mstack = np.concatenate(mats, axis=0)
    return np.concatenate([mstack] * 3, axis=1), np.stack(masks)


HGRN2_HEAD_GROUP = 4


def _hgrn2_kernel(x_ref, g_pre_ref, g_post_ref, win_ref, lb_ref, ng_ref, wout_ref, mstack_ref, mask_ref,
                  o_ref, u_scr, proj_scr, y_scr, q_scr, k_scr, xall_scr, st_scr, *, heads, dk, chunk):
    tb = x_ref.shape[1]
    hk = heads * dk
    n_lvl = len(_level_sizes(chunk))

    @pl.when(pl.program_id(1) == 0)
    def _():
        st_scr[...] = jnp.zeros_like(st_scr)

    u_scr[...] = _rms(x_ref[0], g_pre_ref[...]).astype(BF16)
    for j in range(4):
        proj_scr[:, j * hk:(j + 1) * hk] = _dot(u_scr[...], win_ref[:, j * hk:(j + 1) * hk])

    lb = lb_ref[...]
    log_lb = jnp.log(lb)
    log1m_lb = jnp.log1p(-lb)

    def chunk_body(c, carry):
        r0 = pl.multiple_of(c * chunk, chunk)
        rows = pl.ds(r0, chunk)
        ff = proj_scr[rows, hk:2 * hk]
        q_scr[...] = _silu(proj_scr[rows, 0:hk]) * (dk ** -0.5)
        k_scr[...] = (1.0 - lb) * _sigmoid(-ff)
        a = log1m_lb + _log_sigmoid(ff)
        log_f = jnp.maximum(log_lb, a) + jnp.log1p(jnp.exp(-jnp.abs(log_lb - a)))
        xall_scr[...] = _dot(mstack_ref[...], _split3(log_f))

        for h0 in range(0, heads, HGRN2_HEAD_GROUP):
            hs = range(h0, min(h0 + HGRN2_HEAD_GROUP, heads))
            cols = {h: slice(h * dk, (h + 1) * dk) for h in hs}
            attn = {}
            for lvl in range(n_lvl + 1):
                for h in hs:
                    q = q_scr[:, cols[h]]
                    k = k_scr[:, cols[h]]
                    if lvl < n_lvl:
                        e = jnp.exp(xall_scr[(lvl + 1) * chunk:(lvl + 2) * chunk, cols[h]])
                        q, k = q * e, k * e
                    prod = _dot_nt(q.astype(BF16), k.astype(BF16)) * mask_ref[lvl]
                    attn[h] = prod if lvl == 0 else attn[h] + prod
            g = {h: xall_scr[0:chunk, cols[h]] for h in hs}
            v = {h: proj_scr[rows, 2 * hk + h * dk:2 * hk + (h + 1) * dk].astype(BF16) for h in hs}
            st = {h: st_scr[h] for h in hs}
            o_inter = {h: _dot_nt((q_scr[:, cols[h]] * jnp.exp(g[h])).astype(BF16), st[h].astype(BF16))
                       for h in hs}
            kv = {}
            for h in hs:
                g_last = g[h][chunk - 1:chunk, :]
                k_dec = (k_scr[:, cols[h]] * jnp.exp(g_last - g[h])).astype(BF16)
                kv[h] = _dot_tn(v[h], k_dec)
            o_intra = {h: _dot(attn[h].astype(BF16), v[h]) for h in hs}
            for h in hs:
                st_scr[h] = st[h] * jnp.exp(g[h][chunk - 1:chunk, :]) + kv[h]
                o = o_inter[h] + o_intra[h]
                gate = proj_scr[rows, 3 * hk + h * dk:3 * hk + (h + 1) * dk]
                y = (o * lax.rsqrt(jnp.mean(o * o, axis=-1, keepdims=True) + EPS) * ng_ref[:, cols[h]]
                     * _silu(gate))
                y_scr[rows, cols[h]] = y.astype(BF16)
        return carry

    lax.fori_loop(0, tb // chunk, chunk_body, 0)
    out = _dot(y_scr[...], wout_ref[...])
    o_ref[0] = x_ref[0] + _rms(out, g_post_ref[...])


def _hgrn2_mixer(x, g_pre, g_post, w_in, lower_bound, norm_gain, w_out, *, heads, block=512, chunk=64):
    B, S, D = x.shape
    hk = w_out.shape[0]
    dk = hk // heads
    tb = min(block, S)
    assert S % tb == 0 and tb % chunk == 0 and w_in.shape[1] == 4 * hk
    mstack, masks = _hgrn2_tables(chunk)
    n_rows = mstack.shape[0]
    kern = functools.partial(_hgrn2_kernel, heads=heads, dk=dk, chunk=chunk)
    return pl.pallas_call(
        kern,
        grid=(B, S // tb),
        in_specs=[
            pl.BlockSpec((1, tb, D), lambda b, s: (b, s, 0)),
            _resident((1, D)), _resident((1, D)),
            _resident((D, 4 * hk)), _resident((1, hk)), _resident((1, hk)), _resident((hk, D)),
            _resident(mstack.shape), _resident(masks.shape),
        ],
        out_specs=pl.BlockSpec((1, tb, D), lambda b, s: (b, s, 0)),
        out_shape=jax.ShapeDtypeStruct((B, S, D), F32),
        scratch_shapes=[
            pltpu.VMEM((tb, D), BF16),
            pltpu.VMEM((tb, 4 * hk), F32),
            pltpu.VMEM((tb, hk), BF16),
            pltpu.VMEM((chunk, hk), F32),
            pltpu.VMEM((chunk, hk), F32),
            pltpu.VMEM((n_rows, hk), F32),
            pltpu.VMEM((heads, dk, dk), F32),
        ],
        compiler_params=pltpu.CompilerParams(
            dimension_semantics=("arbitrary", "arbitrary"), vmem_limit_bytes=VMEM_LIMIT_BYTES),
        name="hgrn2_mixer",
    )(x, g_pre.reshape(1, D), g_post.reshape(1, D), w_in.astype(BF16), lower_bound.reshape(1, hk),
      norm_gain.reshape(1, hk), w_out.astype(BF16), jnp.asarray(mstack, BF16), jnp.asarray(masks, F32))


M_HEADS, M_DK, M_DV = 4, 64, 128
G_HEADS, G_DK, G_DV = 4, 128, 128
M_QK = M_HEADS * M_DK
M_V = M_HEADS * M_DV
G_QK = G_HEADS * G_DK
G_V = G_HEADS * G_DV
AB_CHUNK = 64
AB_GROUP = 4
CONV_K = 4
GATE_SOFTCAP = 15.0
GATE_LANES = 128
OFF_MQ, OFF_MK = 0, M_QK
OFF_MV = 2 * M_QK
OFF_MO = OFF_MV + M_V
OFF_GQ = OFF_MO + M_V
OFF_GK = OFF_GQ + G_QK
OFF_GV = OFF_GK + G_QK
OFF_GG = OFF_GV + G_V
OFF_GATES = OFF_GG + G_V
AB_COLS = OFF_GATES + GATE_LANES
LANE_I, LANE_F, LANE_A, LANE_B = 0, M_HEADS, 2 * M_HEADS, 2 * M_HEADS + G_HEADS
ACT_MQ, ACT_MK = 0, M_QK
ACT_GQ = 2 * M_QK
ACT_GK = ACT_GQ + G_QK
ACT_GV = ACT_GK + G_QK
ACT_COLS = ACT_GV + G_V
N_MVEC = 3
N_MSC = 2


def _ab_tables(chunk):
    import numpy as np
    idx = np.arange(chunk)
    l, s = idx[:, None], idx[None, :]
    incl = l >= s
    strict = l > s
    blk16 = (l // 16) == (s // 16)
    blk32 = (l // 32) == (s // 32)
    masks = np.stack([incl, strict, blk16, blk32 & ~blk16, ~blk32, l == s]).astype(np.float32)
    tri3 = np.concatenate([incl.astype(np.float32)] * 3, axis=1)
    return tri3, masks


def _bcast_col(a, lane_ids, j):
    col = jnp.sum(jnp.where(lane_ids == j, a, 0.0), axis=-1, keepdims=True)
    return jnp.broadcast_to(col, a.shape)


def _unit_lower_inverses(ns, blk16, off32, off64, eye):
    d = [n * blk16 for n in ns]
    db = [_bf(x) for x in d]
    d2 = [_dot(x, x) for x in db]
    d2b = [_bf(x) for x in d2]
    p = [_dot(_bf(eye - a), _bf(eye + b)) for a, b in zip(d, d2)]
    d4 = [_dot(x, x) for x in d2b]
    d4b = [_bf(x) for x in d4]
    p = [_dot(_bf(a), _bf(eye + b)) for a, b in zip(p, d4)]
    d8 = [_dot(x, x) for x in d4b]
    p = [_dot(_bf(a), _bf(eye + b)) for a, b in zip(p, d8)]
    for off in (off32, off64):
        pb = [_bf(x) for x in p]
        t = [_dot(a, _bf(n * off)) for a, n in zip(pb, ns)]
        t = [_dot(_bf(a), b) for a, b in zip(t, pb)]
        p = [a - b for a, b in zip(p, t)]
    return p


def _ab_kernel(x_ref, g_pre_ref, g_post_ref, win_ref, cm_ref, cg_ref, gbias_ref, alog_ref, mn_ref, gn_ref,
               wout_ref, tri3_ref, mask_ref, o_ref,
               u_scr, proj_scr, act_scr, y_scr,
               mq_scr, mk_scr, mqk_scr, mlogd_scr, mvec_scr, msc_scr,
               gu_scr, gw_scr, gqd_scr, gkd_scr, gattn_scr, ggl_scr,
               cn_scr, m_scr, s_scr, *, chunk, group):
    tb = x_ref.shape[1]
    n_chunks = tb // chunk

    @pl.when(pl.program_id(1) == 0)
    def _():
        proj_scr[0:HALO, :] = jnp.zeros((HALO, AB_COLS), F32)
        cn_scr[...] = jnp.zeros_like(cn_scr)
        m_scr[...] = jnp.zeros_like(m_scr)
        s_scr[...] = jnp.zeros_like(s_scr)

    u_scr[...] = _bf(_rms(x_ref[0], g_pre_ref[...]))
    for c0 in range(0, AB_COLS, 512):
        c1 = min(c0 + 512, AB_COLS)
        proj_scr[HALO:HALO + tb, c0:c1] = _dot(u_scr[...], win_ref[:, c0:c1])

    def conv_silu(r0, col0, width, w_ref, wcol0):
        acc = None
        for j in range(CONV_K):
            xs = proj_scr[r0 - (CONV_K - 1) + j:r0 - (CONV_K - 1) + j + chunk, col0:col0 + width]
            t = w_ref[j:j + 1, wcol0:wcol0 + width] * xs
            acc = t if acc is None else acc + t
        return _silu(acc)

    def unit(v):
        return v * lax.rsqrt(jnp.sum(v * v, axis=-1, keepdims=True) + EPS)

    for ci in range(n_chunks):
        r0 = HALO + ci * chunk
        a0 = ci * chunk
        mq = conv_silu(r0, OFF_MQ, M_QK, cm_ref, 0)
        act_scr[a0:a0 + chunk, ACT_MQ:ACT_MQ + M_QK] = mq * (M_DK ** -0.5)
        act_scr[a0:a0 + chunk, ACT_MK:ACT_MK + M_QK] = conv_silu(r0, OFF_MK, M_QK, cm_ref, M_QK)
        for h in range(G_HEADS):
            gq = conv_silu(r0, OFF_GQ + h * G_DK, G_DK, cg_ref, h * G_DK)
            act_scr[a0:a0 + chunk, ACT_GQ + h * G_DK:ACT_GQ + (h + 1) * G_DK] = unit(gq) * (G_DK ** -0.5)
            gk = conv_silu(r0, OFF_GK + h * G_DK, G_DK, cg_ref, G_QK + h * G_DK)
            act_scr[a0:a0 + chunk, ACT_GK + h * G_DK:ACT_GK + (h + 1) * G_DK] = unit(gk)
        act_scr[a0:a0 + chunk, ACT_GV:ACT_GV + G_V] = conv_silu(r0, OFF_GV, G_V, cg_ref, 2 * G_QK)
    proj_scr[0:HALO, :] = proj_scr[tb:tb + HALO, :]

    lane = lax.broadcasted_iota(jnp.int32, (chunk, GATE_LANES), 1)
    gbias = gbias_ref[...]
    neg_a = -jnp.exp(alog_ref[...])

    def prep_body(g, carry):
        m_incl, m_strict, m_blk16, m_off32, m_off64, m_eye = (mask_ref[i] for i in range(6))
        ns, betas, gbs, ks, vs, arows = [], [], [], [], [], []
        for i in range(group):
            c = g * group + i
            a0 = pl.multiple_of(c * chunk, chunk)
            arow = pl.ds(a0, chunk)
            prow = pl.ds(a0 + HALO, chunk)

            z = proj_scr[prow, OFF_GATES:OFF_GATES + GATE_LANES] + gbias
            capped = GATE_SOFTCAP * jnp.tanh(z / GATE_SOFTCAP)
            softplus = jnp.maximum(z, 0.0) + jnp.log1p(jnp.exp(-jnp.abs(z)))
            gp = jnp.where(lane < LANE_F, capped,
                           jnp.where(lane < LANE_A, _log_sigmoid(capped),
                                     jnp.where(lane < LANE_B, neg_a * softplus, _sigmoid(z))))
            cs = _dot(tri3_ref[...], _split3(gp))
            gp_t = gp.T
            cs_t = cs.T

            for h in range(M_HEADS):
                q = _bf(act_scr[arow, ACT_MQ + h * M_DK:ACT_MQ + (h + 1) * M_DK])
                k = act_scr[arow, ACT_MK + h * M_DK:ACT_MK + (h + 1) * M_DK]
                b_b = _bcast_col(cs, lane, LANE_F + h)
                ig_b = _bcast_col(gp, lane, LANE_I + h)
                b_row = cs_t[LANE_F + h:LANE_F + h + 1, :]
                ig_row = gp_t[LANE_I + h:LANE_I + h + 1, :]
                log_d = jnp.where(m_incl > 0.0, b_b[:, 0:chunk] - b_row + ig_row, -jnp.inf)
                b_last = b_b[chunk - 1:chunk, :]
                log_w = b_last - b_b + ig_b
                mq_scr[h, arow, :] = q
                mk_scr[h, arow, :] = k
                mqk_scr[h, arow, :] = _dot_nt(q, _bf(k))
                mlogd_scr[h, arow, :] = log_d
                mvec_scr[h * N_MVEC + 0, arow, :] = b_b
                mvec_scr[h * N_MVEC + 1, arow, :] = jnp.broadcast_to(
                    jnp.max(log_d, axis=-1, keepdims=True), (chunk, GATE_LANES))
                mvec_scr[h * N_MVEC + 2, arow, :] = log_w
                msc_scr[c, h * N_MSC + 0] = jnp.broadcast_to(b_last, (HALO, GATE_LANES))
                msc_scr[c, h * N_MSC + 1] = jnp.broadcast_to(
                    jnp.max(log_w, axis=0, keepdims=True), (HALO, GATE_LANES))

            for h in range(G_HEADS):
                q = act_scr[arow, ACT_GQ + h * G_DK:ACT_GQ + (h + 1) * G_DK]
                k = act_scr[arow, ACT_GK + h * G_DK:ACT_GK + (h + 1) * G_DK]
                v = act_scr[arow, ACT_GV + h * G_DV:ACT_GV + (h + 1) * G_DV]
                g_b = _bcast_col(cs, lane, LANE_A + h)
                beta = _bcast_col(gp, lane, LANE_B + h)
                g_row = cs_t[LANE_A + h:LANE_A + h + 1, :]
                decay = jnp.where(m_incl > 0.0,
                                  jnp.exp(jnp.where(m_incl > 0.0, g_b[:, 0:chunk] - g_row, 0.0)), 0.0)
                kb = _bf(k)
                ns.append(m_strict * (beta[:, 0:chunk] * _dot_nt(kb, kb) * decay))
                gattn_scr[h, arow, :] = _bf(_dot_nt(_bf(q), kb) * decay)
                g_last = g_b[chunk - 1:chunk, :]
                gqd_scr[h, arow, :] = _bf(q * jnp.exp(g_b))
                gkd_scr[h, arow, :] = _bf(k * jnp.exp(g_last - g_b))
                ggl_scr[c, h] = jnp.broadcast_to(jnp.exp(g_last), (HALO, GATE_LANES))
                betas.append(beta)
                gbs.append(g_b)
                ks.append(k)
                vs.append(v)
                arows.append((h, arow))

        t_inv = _unit_lower_inverses(ns, m_blk16, m_off32, m_off64, m_eye)
        for t, beta, g_b, k, v, (h, arow) in zip(t_inv, betas, gbs, ks, vs, arows):
            rhs = jnp.concatenate([v * beta, k * (beta * jnp.exp(g_b))], axis=1)
            sol = _dot(_bf(t), _bf(rhs))
            gu_scr[h, arow, :] = sol[:, 0:G_DV]
            gw_scr[h, arow, :] = _bf(sol[:, G_DV:G_DV + G_DK])
        return carry

    lax.fori_loop(0, n_chunks // group, prep_body, 0)

    ones_lane0 = _bf(jnp.where(lax.broadcasted_iota(jnp.int32, (chunk, M_DV), 1) == 0, 1.0, 0.0))

    def seq_body(c, carry):
        a0 = pl.multiple_of(c * chunk, chunk)
        arow = pl.ds(a0, chunk)
        prow = pl.ds(a0 + HALO, chunk)

        m_t, w_inter, s, v_ext, kw, c_decay, m_new = [], [], [], [], [], [], []
        for h in range(M_HEADS):
            m_prev = m_scr[h, 0:1, :]
            b_b = mvec_scr[h * N_MVEC + 0, arow, :]
            b_last = msc_scr[c, h * N_MSC + 0, 0:1, :]
            log_inter = b_b + m_prev
            mt = jnp.maximum(mvec_scr[h * N_MVEC + 1, arow, :], log_inter)
            mn = jnp.maximum(b_last + m_prev, msc_scr[c, h * N_MSC + 1, 0:1, :])
            wk = jnp.exp(mvec_scr[h * N_MVEC + 2, arow, :] - mn)
            m_t.append(mt)
            w_inter.append(jnp.exp(log_inter - mt))
            s.append(_bf(mqk_scr[h, arow, :] * jnp.exp(mlogd_scr[h, arow, :] - mt[:, 0:chunk])))
            v = _bf(proj_scr[prow, OFF_MV + h * M_DV:OFF_MV + (h + 1) * M_DV])
            v_ext.append(jnp.concatenate([v, ones_lane0], axis=1))
            kw.append(_bf(mk_scr[h, arow, :] * wk[:, 0:M_DK]))
            c_decay.append(jnp.exp(b_last + m_prev - mn))
            m_new.append(mn)

        cn = [cn_scr[h] for h in range(M_HEADS)]
        st = [s_scr[h] for h in range(G_HEADS)]
        wq = [_dot(jnp.concatenate([gw_scr[h, arow, :], gqd_scr[h, arow, :]], axis=0), _bf(st[h]))
              for h in range(G_HEADS)]
        sv = [_dot(s[h], v_ext[h]) for h in range(M_HEADS)]
        qc = [_dot(mq_scr[h, arow, :], _bf(cn[h])) for h in range(M_HEADS)]
        upd = [_dot_tn(kw[h], v_ext[h]) for h in range(M_HEADS)]
        v_new = [_bf(gu_scr[h, arow, :] - wq[h][0:chunk]) for h in range(G_HEADS)]
        o_g = [wq[h][chunk:2 * chunk] + _dot(gattn_scr[h, arow, :], v_new[h]) for h in range(G_HEADS)]
        s_upd = [_dot_tn(gkd_scr[h, arow, :], v_new[h]) for h in range(G_HEADS)]

        for h in range(M_HEADS):
            wi = w_inter[h]
            nd = sv[h] + jnp.concatenate([wi, wi], axis=1) * qc[h]
            den = jnp.broadcast_to(nd[:, M_DV:M_DV + 1], (chunk, M_DV))
            hm = nd[:, 0:M_DV] / jnp.maximum(jnp.abs(den), jnp.exp(-m_t[h]))
            cd = c_decay[h]
            cn_scr[h] = jnp.concatenate([cd, cd], axis=1) * cn[h] + upd[h]
            m_scr[h] = jnp.broadcast_to(m_new[h], (HALO, GATE_LANES))
            gate = proj_scr[prow, OFF_MO + h * M_DV:OFF_MO + (h + 1) * M_DV]
            hn = hm * lax.rsqrt(jnp.mean(hm * hm, axis=-1, keepdims=True) + EPS)
            y_scr[arow, h * M_DV:(h + 1) * M_DV] = _bf(
                _sigmoid(gate) * hn * mn_ref[:, h * M_DV:(h + 1) * M_DV])

        for h in range(G_HEADS):
            s_scr[h] = ggl_scr[c, h, 0:1, :] * st[h] + s_upd[h]
            o = o_g[h]
            gate = proj_scr[prow, OFF_GG + h * G_DV:OFF_GG + (h + 1) * G_DV]
            on = o * lax.rsqrt(jnp.mean(o * o, axis=-1, keepdims=True) + EPS)
            y_scr[arow, M_V + h * G_DV:M_V + (h + 1) * G_DV] = _bf(
                on * gn_ref[:, h * G_DV:(h + 1) * G_DV] * _silu(gate))
        return carry

    lax.fori_loop(0, n_chunks, seq_body, 0)
    out = _dot(y_scr[...], wout_ref[...])
    o_ref[0] = x_ref[0] + _rms(out, g_post_ref[...])


def _ab_pack_w_in(w_in):
    sizes = (M_QK, M_QK, M_V, M_V, M_HEADS, M_HEADS, G_QK, G_QK, G_V, G_V, G_HEADS, G_HEADS)
    offs = [0]
    for sz in sizes:
        offs.append(offs[-1] + sz)
    seg = [w_in[:, offs[i]:offs[i + 1]] for i in range(len(sizes))]
    m_q, m_k, m_v, m_o, m_i, m_f, g_q, g_k, g_v, g_gate, g_a, g_b = seg
    gates = jnp.concatenate([m_i, m_f, g_a, g_b], axis=1)
    gates = jnp.pad(gates, ((0, 0), (0, GATE_LANES - gates.shape[1])))
    return jnp.concatenate([m_q, m_k, m_v, m_o, g_q, g_k, g_v, g_gate, gates], axis=1).astype(BF16)


def _lane_vec(pieces):
    v = jnp.zeros((GATE_LANES,), F32)
    for off, vals in pieces:
        v = lax.dynamic_update_slice(v, vals.astype(F32), (off,))
    return v.reshape(1, GATE_LANES)


def _ab_mixer(x, g_pre, g_post, w_in, conv_m, m_gate_bias, m_norm, conv_g, g_a_log, g_dt_bias, g_norm, w_out,
              *, block=512):
    B, S, D = x.shape
    chunk = AB_CHUNK
    tb = min(block, S)
    n_chunks = tb // chunk
    group = min(AB_GROUP, n_chunks)
    assert S % tb == 0 and tb % chunk == 0 and n_chunks % group == 0
    tri3, masks = _ab_tables(chunk)
    gbias = _lane_vec([(LANE_I, m_gate_bias[0]), (LANE_F, m_gate_bias[1]), (LANE_A, g_dt_bias)])
    alog = _lane_vec([(LANE_A, g_a_log)])
    kern = functools.partial(_ab_kernel, chunk=chunk, group=group)
    mix = M_V + G_V
    return pl.pallas_call(
        kern,
        grid=(B, S // tb),
        in_specs=[
            pl.BlockSpec((1, tb, D), lambda b, s: (b, s, 0)),
            _resident((1, D)), _resident((1, D)),
            _resident((D, AB_COLS)), _resident(conv_m.shape), _resident(conv_g.shape),
            _resident((1, GATE_LANES)), _resident((1, GATE_LANES)),
            _resident((1, M_V)), _resident((1, G_V)), _resident((mix, D)),
            _resident(tri3.shape), _resident(masks.shape),
        ],
        out_specs=pl.BlockSpec((1, tb, D), lambda b, s: (b, s, 0)),
        out_shape=jax.ShapeDtypeStruct((B, S, D), F32),
        scratch_shapes=[
            pltpu.VMEM((tb, D), BF16),
            pltpu.VMEM((HALO + tb, AB_COLS), F32),
            pltpu.VMEM((tb, ACT_COLS), F32),
            pltpu.VMEM((tb, mix), BF16),
            pltpu.VMEM((M_HEADS, tb, M_DK), BF16),
            pltpu.VMEM((M_HEADS, tb, M_DK), F32),
            pltpu.VMEM((M_HEADS, tb, chunk), F32),
            pltpu.VMEM((M_HEADS, tb, chunk), F32),
            pltpu.VMEM((M_HEADS * N_MVEC, tb, GATE_LANES), F32),
            pltpu.VMEM((n_chunks, M_HEADS * N_MSC, HALO, GATE_LANES), F32),
            pltpu.VMEM((G_HEADS, tb, G_DV), F32),
            pltpu.VMEM((G_HEADS, tb, G_DK), BF16),
            pltpu.VMEM((G_HEADS, tb, G_DK), BF16),
            pltpu.VMEM((G_HEADS, tb, G_DK), BF16),
            pltpu.VMEM((G_HEADS, tb, chunk), BF16),
            pltpu.VMEM((n_chunks, G_HEADS, HALO, GATE_LANES), F32),
            pltpu.VMEM((M_HEADS, M_DK, M_DV + 128), F32),
            pltpu.VMEM((M_HEADS, HALO, GATE_LANES), F32),
            pltpu.VMEM((G_HEADS, G_DK, G_DV), F32),
        ],
        compiler_params=pltpu.CompilerParams(
            dimension_semantics=("arbitrary", "arbitrary"), vmem_limit_bytes=VMEM_LIMIT_BYTES),
        name="ab_mixer",
    )(x, g_pre.reshape(1, D), g_post.reshape(1, D), _ab_pack_w_in(w_in), conv_m, conv_g, gbias, alog,
      m_norm.reshape(1, M_V), g_norm.reshape(1, G_V), w_out.astype(BF16),
      jnp.asarray(tri3, BF16), jnp.asarray(masks, F32))


def kernel(x, norm_gains, ab_w_in, ab_conv_m, ab_m_gate_bias, ab_m_norm, ab_conv_g, ab_g_a_log, ab_g_dt_bias, ab_g_norm, ab_w_out, c_w_in, c_lb_logits, c_norm, c_w_out, ffn_w_up, ffn_conv_w, ffn_conv_b, ffn_w_down):
    depth = norm_gains.shape[0]
    p = jax.nn.softmax(c_lb_logits.astype(F32), axis=0)
    lower_bounds = jnp.cumsum(p, axis=0) - p[0]
    c_heads = c_w_out.shape[1] // 128
    for layer in range(depth):
        j = layer // 2
        g = norm_gains[layer]
        if layer % 2 == 0:
            x = _ab_mixer(x, g[0], g[1], ab_w_in[j], ab_conv_m[j], ab_m_gate_bias[j], ab_m_norm[j],
                          ab_conv_g[j], ab_g_a_log[j], ab_g_dt_bias[j], ab_g_norm[j], ab_w_out[j])
        else:
            x = _hgrn2_mixer(x, g[0], g[1], c_w_in[j], lower_bounds[layer], c_norm[j], c_w_out[j],
                             heads=c_heads)
        x = _ffn(x, g[2], g[3], ffn_w_up[layer], ffn_conv_w[layer], ffn_conv_b[layer], ffn_w_down[layer])
    return x
```

```python
import functools

import jax
import jax.numpy as jnp
from jax import lax
from jax.experimental import pallas as pl
from jax.experimental.pallas import tpu as pltpu

F32 = jnp.float32
BF16 = jnp.bfloat16
EPS = 1e-6
LANES = 128
HALO = 8
FFN_DOWN_LAG = 2
FFN_DOWN_PART = 4
VMEM_LIMIT_BYTES = 56 * 1024 * 1024


def _rms(x, gain):
    return x * lax.rsqrt(jnp.mean(x * x, axis=-1, keepdims=True) + EPS) * gain


def _sigmoid(x):
    return 0.5 * jnp.tanh(0.5 * x) + 0.5


def _silu(x):
    return x * _sigmoid(x)


def _log1p_exp_neg_abs(x):
    return jnp.log(1.0 + jnp.exp(-jnp.abs(x)))


def _resident(shape):
    nd = len(shape)
    return pl.BlockSpec(shape, lambda b, s: (0,) * nd, pipeline_mode=pl.Buffered(1))


def _ffn_kernel(x_ref, g_pre_ref, g_post_ref, wup_ref, cw_ref, cb_ref, wdn_ref, o_ref,
                u_scr, a_scr, h_scr, carry_scr, *, d_ff, slab):
    tb = x_ref.shape[1]

    @pl.when(pl.program_id(1) == 0)
    def _():
        carry_scr[...] = jnp.zeros_like(carry_scr)

    u_scr[...] = _rms(x_ref[0], g_pre_ref[...]).astype(BF16)
    n_slab = d_ff // slab

    def conv_half(buf, c0):
        h = jnp.dot(u_scr[...], wup_ref[:, c0:c0 + slab], preferred_element_type=F32)
        outs = []
        for lt in range(slab // LANES):
            cols = slice(c0 + lt * LANES, c0 + (lt + 1) * LANES)
            hl = h[:, lt * LANES:(lt + 1) * LANES]
            h_scr[buf, lt, 0:HALO, :] = carry_scr[:, cols]
            h_scr[buf, lt, HALO:HALO + tb, :] = hl
            carry_scr[:, cols] = hl[tb - HALO:, :]
            w = cw_ref[:, cols]
            outs.append(w[0:1] * h_scr[buf, lt, HALO - 2:HALO - 2 + tb, :]
                        + w[1:2] * h_scr[buf, lt, HALO - 1:HALO - 1 + tb, :]
                        + w[2:3] * hl + cb_ref[:, cols])
        return outs

    out, done = None, 0
    for j in range(n_slab):
        gate = conv_half(2 * (j % 2), j * slab)
        val = conv_half(2 * (j % 2) + 1, d_ff + j * slab)
        for lt, (g, v) in enumerate(zip(gate, val)):
            a_scr[:, j * slab + lt * LANES:j * slab + (lt + 1) * LANES] = (_silu(g) * v).astype(BF16)
        ready = j + 1 - FFN_DOWN_LAG
        if ready - done >= FFN_DOWN_PART or (j == n_slab - 1 and ready > done):
            part = jnp.dot(a_scr[:, done * slab:ready * slab], wdn_ref[done * slab:ready * slab, :],
                           preferred_element_type=F32)
            out = part if out is None else out + part
            done = ready
    if done < n_slab:
        part = jnp.dot(a_scr[:, done * slab:], wdn_ref[done * slab:, :], preferred_element_type=F32)
        out = part if out is None else out + part
    o_ref[0] = x_ref[0] + _rms(out, g_post_ref[...])


def _ffn(x, g_pre, g_post, w_up, conv_w, conv_b, w_down, *, block=512, slab=256):
    B, S, D = x.shape
    d_ff = w_down.shape[0]
    tb = min(block, S)
    assert S % tb == 0 and d_ff % slab == 0
    kern = functools.partial(_ffn_kernel, d_ff=d_ff, slab=slab)
    return pl.pallas_call(
        kern,
        grid=(B, S // tb),
        in_specs=[
            pl.BlockSpec((1, tb, D), lambda b, s: (b, s, 0)),
            _resident((1, D)), _resident((1, D)),
            _resident((D, 2 * d_ff)), _resident((3, 2 * d_ff)), _resident((1, 2 * d_ff)),
            _resident((d_ff, D)),
        ],
        out_specs=pl.BlockSpec((1, tb, D), lambda b, s: (b, s, 0)),
        out_shape=jax.ShapeDtypeStruct((B, S, D), F32),
        scratch_shapes=[
            pltpu.VMEM((tb, D), BF16),
            pltpu.VMEM((tb, d_ff), BF16),
            pltpu.VMEM((4, slab // LANES, HALO + tb, LANES), F32),
            pltpu.VMEM((HALO, 2 * d_ff), F32),
        ],
        compiler_params=pltpu.CompilerParams(
            dimension_semantics=("arbitrary", "arbitrary"), vmem_limit_bytes=VMEM_LIMIT_BYTES),
        name="ffn",
    )(x, g_pre.reshape(1, D), g_post.reshape(1, D), w_up.astype(BF16), conv_w,
      conv_b.reshape(1, 2 * d_ff), w_down.astype(BF16))


def _dot(a, b):
    return jnp.dot(a, b, preferred_element_type=F32)


def _dot_nt(a, b):
    return lax.dot_general(a, b, (((1,), (1,)), ((), ())), preferred_element_type=F32)


def _dot_tn(a, b):
    return lax.dot_general(a, b, (((0,), (0,)), ((), ())), preferred_element_type=F32)


def _bf(x):
    return x.astype(BF16)


def _split3(x):
    hi = x.astype(BF16)
    r1 = x - hi.astype(F32)
    mid = r1.astype(BF16)
    lo = (r1 - mid.astype(F32)).astype(BF16)
    return jnp.concatenate([hi, mid, lo], axis=0)


def _log_sigmoid(x):
    return jnp.minimum(x, 0.0) - _log1p_exp_neg_abs(x)


def _level_sizes(chunk):
    sizes, m = [], chunk // 2
    while m >= 1:
        sizes.append(m)
        m //= 2
    return sizes


def _hgrn2_tables(chunk):
    import numpy as np
    L = chunk
    idx = np.arange(L)
    mats = [(idx[None, :] <= idx[:, None]).astype(np.float32)]
    masks = []
    for m in _level_sizes(L):
        in_b = (idx % (2 * m)) >= m
        blk_start = (idx // m) * m
        blk_end = blk_start + m - 1
        c = idx[None, :]
        mat_b = (c >= blk_start[:, None]) & (c <= idx[:, None])
        mat_a = (c > idx[:, None]) & (c <= blk_end[:, None])
        mats.append(np.where(in_b[:, None], mat_b, mat_a).astype(np.float32))
        same_pair = (idx[:, None] // (2 * m)) == (idx[None, :] // (2 * m))
        masks.append((same_pair & in_b[:, None] & (~in_b)[None, :]).astype(np.float32))
    masks.append(np.eye(L, dtype=np.float32))
    mstack = np.concatenate(mats, axis=0)
    return np.concatenate([mstack] * 3, axis=1), np.stack(masks)


HGRN2_HEAD_GROUP = 4
HGRN2_UNROLL = 2


def _hgrn2_kernel(x_ref, g_pre_ref, g_post_ref, win_ref, lb_ref, ng_ref, wout_ref, mstack_ref, mask_ref,
                  o_ref, u_scr, proj_scr, y_scr, q_scr, k_scr, xall_scr, st_scr, *, heads, dk, chunk, unroll):
    tb = x_ref.shape[1]
    hk = heads * dk
    n_lvl = len(_level_sizes(chunk))

    @pl.when(pl.program_id(1) == 0)
    def _():
        st_scr[...] = jnp.zeros_like(st_scr)

    u_scr[...] = _rms(x_ref[0], g_pre_ref[...]).astype(BF16)
    for j in range(4):
        proj_scr[:, j * hk:(j + 1) * hk] = _dot(u_scr[...], win_ref[:, j * hk:(j + 1) * hk])

    lb = lb_ref[...]
    log_lb = jnp.log(lb)
    log1m_lb = jnp.log1p(-lb)

    def decay_phase(c, buf):
        rows = pl.ds(pl.multiple_of(c * chunk, chunk), chunk)
        ff = proj_scr[rows, hk:2 * hk]
        q_scr[buf] = _bf(_silu(proj_scr[rows, 0:hk]) * (dk ** -0.5))
        k_scr[buf] = _bf((1.0 - lb) * _sigmoid(-ff))
        a = log1m_lb + _log_sigmoid(ff)
        log_f = jnp.maximum(log_lb, a) + _log1p_exp_neg_abs(log_lb - a)
        xall_scr[buf] = _dot(mstack_ref[...], _split3(log_f))

    def heads_phase(c, buf):
        rows = pl.ds(pl.multiple_of(c * chunk, chunk), chunk)
        for h0 in range(0, heads, HGRN2_HEAD_GROUP):
            hs = range(h0, min(h0 + HGRN2_HEAD_GROUP, heads))
            cols = {h: slice(h * dk, (h + 1) * dk) for h in hs}
            attn = {}
            for lvl in range(n_lvl + 1):
                for h in hs:
                    q = q_scr[buf, :, cols[h]]
                    k = k_scr[buf, :, cols[h]]
                    if lvl < n_lvl:
                        e = _bf(jnp.exp(xall_scr[buf, (lvl + 1) * chunk:(lvl + 2) * chunk, cols[h]]))
                        q, k = q * e, k * e
                    prod = _dot_nt(q, k) * mask_ref[lvl]
                    attn[h] = prod if lvl == 0 else attn[h] + prod
            g = {h: xall_scr[buf, 0:chunk, cols[h]] for h in hs}
            v = {h: proj_scr[rows, 2 * hk + h * dk:2 * hk + (h + 1) * dk].astype(BF16) for h in hs}
            st = {h: st_scr[h] for h in hs}
            o_inter = {h: _dot_nt(q_scr[buf, :, cols[h]] * _bf(jnp.exp(g[h])), st[h].astype(BF16))
                       for h in hs}
            kv = {}
            for h in hs:
                g_last = g[h][chunk - 1:chunk, :]
                k_dec = k_scr[buf, :, cols[h]] * _bf(jnp.exp(g_last - g[h]))
                kv[h] = _dot_tn(v[h], k_dec)
            o_intra = {h: _dot(attn[h].astype(BF16), v[h]) for h in hs}
            for h in hs:
                st_scr[h] = st[h] * jnp.exp(g[h][chunk - 1:chunk, :]) + kv[h]
                o = o_inter[h] + o_intra[h]
                gate = proj_scr[rows, 3 * hk + h * dk:3 * hk + (h + 1) * dk]
                y = (o * lax.rsqrt(jnp.mean(o * o, axis=-1, keepdims=True) + EPS) * ng_ref[:, cols[h]]
                     * _silu(gate))
                y_scr[rows, cols[h]] = y.astype(BF16)

    def chunks_body(i, carry):
        for j in range(unroll):
            decay_phase(i * unroll + j, j)
        for j in range(unroll):
            heads_phase(i * unroll + j, j)
        return carry

    lax.fori_loop(0, tb // chunk // unroll, chunks_body, 0)
    out = _dot(y_scr[...], wout_ref[...])
    o_ref[0] = x_ref[0] + _rms(out, g_post_ref[...])


def _hgrn2_mixer(x, g_pre, g_post, w_in, lower_bound, norm_gain, w_out, *, heads, block=512, chunk=64):
    B, S, D = x.shape
    hk = w_out.shape[0]
    dk = hk // heads
    tb = min(block, S)
    unroll = min(HGRN2_UNROLL, tb // chunk)
    assert S % tb == 0 and tb % (chunk * unroll) == 0 and w_in.shape[1] == 4 * hk
    mstack, masks = _hgrn2_tables(chunk)
    n_rows = mstack.shape[0]
    kern = functools.partial(_hgrn2_kernel, heads=heads, dk=dk, chunk=chunk, unroll=unroll)
    return pl.pallas_call(
        kern,
        grid=(B, S // tb),
        in_specs=[
            pl.BlockSpec((1, tb, D), lambda b, s: (b, s, 0)),
            _resident((1, D)), _resident((1, D)),
            _resident((D, 4 * hk)), _resident((1, hk)), _resident((1, hk)), _resident((hk, D)),
            _resident(mstack.shape), _resident(masks.shape),
        ],
        out_specs=pl.BlockSpec((1, tb, D), lambda b, s: (b, s, 0)),
        out_shape=jax.ShapeDtypeStruct((B, S, D), F32),
        scratch_shapes=[
            pltpu.VMEM((tb, D), BF16),
            pltpu.VMEM((tb, 4 * hk), F32),
            pltpu.VMEM((tb, hk), BF16),
            pltpu.VMEM((unroll, chunk, hk), BF16),
            pltpu.VMEM((unroll, chunk, hk), BF16),
            pltpu.VMEM((unroll, n_rows, hk), F32),
            pltpu.VMEM((heads, dk, dk), F32),
        ],
        compiler_params=pltpu.CompilerParams(
            dimension_semantics=("arbitrary", "arbitrary"), vmem_limit_bytes=VMEM_LIMIT_BYTES),
        name="hgrn2_mixer",
    )(x, g_pre.reshape(1, D), g_post.reshape(1, D), w_in.astype(BF16), lower_bound.reshape(1, hk),
      norm_gain.reshape(1, hk), w_out.astype(BF16), jnp.asarray(mstack, BF16), jnp.asarray(masks, F32))


M_HEADS, M_DK, M_DV = 4, 64, 128
G_HEADS, G_DK, G_DV = 4, 128, 128
M_QK = M_HEADS * M_DK
M_V = M_HEADS * M_DV
G_QK = G_HEADS * G_DK
G_V = G_HEADS * G_DV
AB_CHUNK = 64
AB_GROUP = 4
CONV_K = 4
GATE_SOFTCAP = 15.0
GATE_LANES = 128
CT_MQ = 0
CT_MK = CT_MQ + M_QK // LANES
CT_GQ = CT_MK + M_QK // LANES
CT_GK = CT_GQ + G_QK // LANES
CT_GV = CT_GK + G_QK // LANES
N_CONV_TILES = CT_GV + G_V // LANES
CONV_COLS = N_CONV_TILES * LANES
P_MV = 0
P_MO = P_MV + M_V
P_GG = P_MO + M_V
P_GATES = P_GG + G_V
PLAIN_COLS = P_GATES + GATE_LANES
AB_COLS = CONV_COLS + PLAIN_COLS
LANE_I, LANE_F, LANE_A, LANE_B = 0, M_HEADS, 2 * M_HEADS, 2 * M_HEADS + G_HEADS
ACT_MQ, ACT_MK = 0, M_QK
ACT_GQ = 2 * M_QK
ACT_GK = ACT_GQ + G_QK
ACT_GV = ACT_GK + G_QK
ACT_COLS = ACT_GV + G_V
N_MVEC = 3
N_MSC = 2


def _ab_tables(chunk):
    import numpy as np
    idx = np.arange(chunk)
    l, s = idx[:, None], idx[None, :]
    incl = l >= s
    strict = l > s
    blk16 = (l // 16) == (s // 16)
    blk32 = (l // 32) == (s // 32)
    masks = np.stack([incl, strict, blk16, blk32 & ~blk16, ~blk32, l == s]).astype(np.float32)
    tri3 = np.concatenate([incl.astype(np.float32)] * 3, axis=1)
    return tri3, masks


def _bcast_col(a, lane_ids, j):
    col = jnp.sum(jnp.where(lane_ids == j, a, 0.0), axis=-1, keepdims=True)
    return jnp.broadcast_to(col, a.shape)


def _unit_lower_inverses(ns, blk16, off32, off64, eye):
    d = [n * blk16 for n in ns]
    db = [_bf(x) for x in d]
    d2 = [_dot(x, x) for x in db]
    d2b = [_bf(x) for x in d2]
    p = [_dot(_bf(eye - a), _bf(eye + b)) for a, b in zip(d, d2)]
    d4 = [_dot(x, x) for x in d2b]
    d4b = [_bf(x) for x in d4]
    p = [_dot(_bf(a), _bf(eye + b)) for a, b in zip(p, d4)]
    d8 = [_dot(x, x) for x in d4b]
    p = [_dot(_bf(a), _bf(eye + b)) for a, b in zip(p, d8)]
    for off in (off32, off64):
        pb = [_bf(x) for x in p]
        t = [_dot(a, _bf(n * off)) for a, n in zip(pb, ns)]
        t = [_dot(_bf(a), b) for a, b in zip(t, pb)]
        p = [a - b for a, b in zip(p, t)]
    return p


def _ab_kernel(x_ref, g_pre_ref, g_post_ref, win_ref, cm_ref, cg_ref, gbias_ref, alog_ref, mn_ref, gn_ref,
               wout_ref, tri3_ref, mask_ref, o_ref,
               u_scr, conv_scr, proj_scr, act_scr, y_scr,
               mq_scr, mk_scr, mqk_scr, mlogd_scr, mvec_scr, msc_scr,
               gu_scr, gw_scr, gqd_scr, gkd_scr, gattn_scr, ggl_scr,
               cn_scr, m_scr, s_scr, *, chunk, group):
    tb = x_ref.shape[1]
    n_chunks = tb // chunk

    @pl.when(pl.program_id(1) == 0)
    def _():
        conv_scr[:, 0:HALO, :] = jnp.zeros((N_CONV_TILES, HALO, LANES), F32)
        cn_scr[...] = jnp.zeros_like(cn_scr)
        m_scr[...] = jnp.zeros_like(m_scr)
        s_scr[...] = jnp.zeros_like(s_scr)

    u_scr[...] = _bf(_rms(x_ref[0], g_pre_ref[...]))
    for c0 in range(0, CONV_COLS, 512):
        r = _dot(u_scr[...], win_ref[:, c0:c0 + 512])
        for lt in range(512 // LANES):
            conv_scr[c0 // LANES + lt, HALO:HALO + tb, :] = r[:, lt * LANES:(lt + 1) * LANES]
    for c0 in range(0, PLAIN_COLS, 512):
        c1 = min(c0 + 512, PLAIN_COLS)
        proj_scr[:, c0:c1] = _dot(u_scr[...], win_ref[:, CONV_COLS + c0:CONV_COLS + c1])

    def conv_silu(ci, lt, w_ref, wcol0):
        r0 = HALO + ci * chunk - (CONV_K - 1)
        acc = None
        for j in range(CONV_K):
            t = w_ref[j:j + 1, wcol0:wcol0 + LANES] * conv_scr[lt, r0 + j:r0 + j + chunk, :]
            acc = t if acc is None else acc + t
        return _silu(acc)

    def unit(v):
        return v * lax.rsqrt(jnp.sum(v * v, axis=-1, keepdims=True) + EPS)

    for ci in range(n_chunks):
        arow = slice(ci * chunk, (ci + 1) * chunk)
        for t in range(M_QK // LANES):
            mq = conv_silu(ci, CT_MQ + t, cm_ref, t * LANES)
            act_scr[arow, ACT_MQ + t * LANES:ACT_MQ + (t + 1) * LANES] = mq * (M_DK ** -0.5)
            mk = conv_silu(ci, CT_MK + t, cm_ref, M_QK + t * LANES)
            act_scr[arow, ACT_MK + t * LANES:ACT_MK + (t + 1) * LANES] = mk
        for h in range(G_HEADS):
            gq = conv_silu(ci, CT_GQ + h, cg_ref, h * G_DK)
            act_scr[arow, ACT_GQ + h * G_DK:ACT_GQ + (h + 1) * G_DK] = unit(gq) * (G_DK ** -0.5)
            gk = conv_silu(ci, CT_GK + h, cg_ref, G_QK + h * G_DK)
            act_scr[arow, ACT_GK + h * G_DK:ACT_GK + (h + 1) * G_DK] = unit(gk)
            gv = conv_silu(ci, CT_GV + h, cg_ref, 2 * G_QK + h * G_DV)
            act_scr[arow, ACT_GV + h * G_DV:ACT_GV + (h + 1) * G_DV] = gv
    conv_scr[:, 0:HALO, :] = conv_scr[:, tb:tb + HALO, :]

    lane = lax.broadcasted_iota(jnp.int32, (chunk, GATE_LANES), 1)
    gbias = gbias_ref[...]
    neg_a = -jnp.exp(alog_ref[...])

    def prep_body(g, carry):
        m_incl, m_strict, m_blk16, m_off32, m_off64, m_eye = (mask_ref[i] for i in range(6))
        ns, betas, gbs, ks, vs, arows = [], [], [], [], [], []
        for i in range(group):
            c = g * group + i
            a0 = pl.multiple_of(c * chunk, chunk)
            arow = pl.ds(a0, chunk)

            z = proj_scr[arow, P_GATES:P_GATES + GATE_LANES] + gbias
            capped = GATE_SOFTCAP * jnp.tanh(z / GATE_SOFTCAP)
            softplus = jnp.maximum(z, 0.0) + _log1p_exp_neg_abs(z)
            gp = jnp.where(lane < LANE_F, capped,
                           jnp.where(lane < LANE_A, _log_sigmoid(capped),
                                     jnp.where(lane < LANE_B, neg_a * softplus, _sigmoid(z))))
            cs = _dot(tri3_ref[...], _split3(gp))
            gp_t = gp.T
            cs_t = cs.T

            for h in range(M_HEADS):
                q = _bf(act_scr[arow, ACT_MQ + h * M_DK:ACT_MQ + (h + 1) * M_DK])
                k = act_scr[arow, ACT_MK + h * M_DK:ACT_MK + (h + 1) * M_DK]
                b_b = _bcast_col(cs, lane, LANE_F + h)
                ig_b = _bcast_col(gp, lane, LANE_I + h)
                b_row = cs_t[LANE_F + h:LANE_F + h + 1, :]
                ig_row = gp_t[LANE_I + h:LANE_I + h + 1, :]
                log_d = jnp.where(m_incl > 0.0, b_b[:, 0:chunk] - b_row + ig_row, -jnp.inf)
                b_last = b_b[chunk - 1:chunk, :]
                log_w = b_last - b_b + ig_b
                mq_scr[h, arow, :] = q
                mk_scr[h, arow, :] = k
                mqk_scr[h, arow, :] = _dot_nt(q, _bf(k))
                mlogd_scr[h, arow, :] = log_d
                mvec_scr[h * N_MVEC + 0, arow, :] = b_b
                mvec_scr[h * N_MVEC + 1, arow, :] = jnp.broadcast_to(
                    jnp.max(log_d, axis=-1, keepdims=True), (chunk, GATE_LANES))
                mvec_scr[h * N_MVEC + 2, arow, :] = log_w
                msc_scr[c, h * N_MSC + 0] = jnp.broadcast_to(b_last, (HALO, GATE_LANES))
                msc_scr[c, h * N_MSC + 1] = jnp.broadcast_to(
                    jnp.max(log_w, axis=0, keepdims=True), (HALO, GATE_LANES))

            for h in range(G_HEADS):
                q = act_scr[arow, ACT_GQ + h * G_DK:ACT_GQ + (h + 1) * G_DK]
                k = act_scr[arow, ACT_GK + h * G_DK:ACT_GK + (h + 1) * G_DK]
                v = act_scr[arow, ACT_GV + h * G_DV:ACT_GV + (h + 1) * G_DV]
                g_b = _bcast_col(cs, lane, LANE_A + h)
                beta = _bcast_col(gp, lane, LANE_B + h)
                g_row = cs_t[LANE_A + h:LANE_A + h + 1, :]
                decay = jnp.where(m_incl > 0.0,
                                  jnp.exp(jnp.where(m_incl > 0.0, g_b[:, 0:chunk] - g_row, 0.0)), 0.0)
                kb = _bf(k)
                ns.append(m_strict * (beta[:, 0:chunk] * _dot_nt(kb, kb) * decay))
                gattn_scr[h, arow, :] = _bf(_dot_nt(_bf(q), kb) * decay)
                g_last = g_b[chunk - 1:chunk, :]
                gqd_scr[h, arow, :] = _bf(q * jnp.exp(g_b))
                gkd_scr[h, arow, :] = _bf(k * jnp.exp(g_last - g_b))
                ggl_scr[c, h] = jnp.broadcast_to(jnp.exp(g_last), (HALO, GATE_LANES))
                betas.append(beta)
                gbs.append(g_b)
                ks.append(k)
                vs.append(v)
                arows.append((h, arow))

        t_inv = _unit_lower_inverses(ns, m_blk16, m_off32, m_off64, m_eye)
        for t, beta, g_b, k, v, (h, arow) in zip(t_inv, betas, gbs, ks, vs, arows):
            rhs = jnp.concatenate([v * beta, k * (beta * jnp.exp(g_b))], axis=1)
            sol = _dot(_bf(t), _bf(rhs))
            gu_scr[h, arow, :] = sol[:, 0:G_DV]
            gw_scr[h, arow, :] = _bf(sol[:, G_DV:G_DV + G_DK])
        return carry

    lax.fori_loop(0, n_chunks // group, prep_body, 0)

    ones_lane0 = _bf(jnp.where(lax.broadcasted_iota(jnp.int32, (chunk, M_DV), 1) == 0, 1.0, 0.0))

    def seq_body(c, carry):
        a0 = pl.multiple_of(c * chunk, chunk)
        arow = pl.ds(a0, chunk)

        m_t, w_inter, s, v_ext, kw, c_decay, m_new = [], [], [], [], [], [], []
        for h in range(M_HEADS):
            m_prev = m_scr[h, 0:1, :]
            b_b = mvec_scr[h * N_MVEC + 0, arow, :]
            b_last = msc_scr[c, h * N_MSC + 0, 0:1, :]
            log_inter = b_b + m_prev
            mt = jnp.maximum(mvec_scr[h * N_MVEC + 1, arow, :], log_inter)
            mn = jnp.maximum(b_last + m_prev, msc_scr[c, h * N_MSC + 1, 0:1, :])
            wk = jnp.exp(mvec_scr[h * N_MVEC + 2, arow, :] - mn)
            m_t.append(mt)
            w_inter.append(jnp.exp(log_inter - mt))
            s.append(_bf(mqk_scr[h, arow, :] * jnp.exp(mlogd_scr[h, arow, :] - mt[:, 0:chunk])))
            v = _bf(proj_scr[arow, P_MV + h * M_DV:P_MV + (h + 1) * M_DV])
            v_ext.append(jnp.concatenate([v, ones_lane0], axis=1))
            kw.append(_bf(mk_scr[h, arow, :] * wk[:, 0:M_DK]))
            c_decay.append(jnp.exp(b_last + m_prev - mn))
            m_new.append(mn)

        cn = [cn_scr[h] for h in range(M_HEADS)]
        st = [s_scr[h] for h in range(G_HEADS)]
        wq = [_dot(jnp.concatenate([gw_scr[h, arow, :], gqd_scr[h, arow, :]], axis=0), _bf(st[h]))
              for h in range(G_HEADS)]
        sv = [_dot(s[h], v_ext[h]) for h in range(M_HEADS)]
        qc = [_dot(mq_scr[h, arow, :], _bf(cn[h])) for h in range(M_HEADS)]
        upd = [_dot_tn(kw[h], v_ext[h]) for h in range(M_HEADS)]
        v_new = [_bf(gu_scr[h, arow, :] - wq[h][0:chunk]) for h in range(G_HEADS)]
        o_g = [wq[h][chunk:2 * chunk] + _dot(gattn_scr[h, arow, :], v_new[h]) for h in range(G_HEADS)]
        s_upd = [_dot_tn(gkd_scr[h, arow, :], v_new[h]) for h in range(G_HEADS)]

        for h in range(M_HEADS):
            wi = w_inter[h]
            nd = sv[h] + jnp.concatenate([wi, wi], axis=1) * qc[h]
            den = jnp.broadcast_to(nd[:, M_DV:M_DV + 1], (chunk, M_DV))
            hm = nd[:, 0:M_DV] / jnp.maximum(jnp.abs(den), jnp.exp(-m_t[h]))
            cd = c_decay[h]
            cn_scr[h] = jnp.concatenate([cd, cd], axis=1) * cn[h] + upd[h]
            m_scr[h] = jnp.broadcast_to(m_new[h], (HALO, GATE_LANES))
            gate = proj_scr[arow, P_MO + h * M_DV:P_MO + (h + 1) * M_DV]
            hn = hm * lax.rsqrt(jnp.mean(hm * hm, axis=-1, keepdims=True) + EPS)
            y_scr[arow, h * M_DV:(h + 1) * M_DV] = _bf(
                _sigmoid(gate) * hn * mn_ref[:, h * M_DV:(h + 1) * M_DV])

        for h in range(G_HEADS):
            s_scr[h] = ggl_scr[c, h, 0:1, :] * st[h] + s_upd[h]
            o = o_g[h]
            gate = proj_scr[arow, P_GG + h * G_DV:P_GG + (h + 1) * G_DV]
            on = o * lax.rsqrt(jnp.mean(o * o, axis=-1, keepdims=True) + EPS)
            y_scr[arow, M_V + h * G_DV:M_V + (h + 1) * G_DV] = _bf(
                on * gn_ref[:, h * G_DV:(h + 1) * G_DV] * _silu(gate))
        return carry

    lax.fori_loop(0, n_chunks, seq_body, 0)
    out = _dot(y_scr[...], wout_ref[...])
    o_ref[0] = x_ref[0] + _rms(out, g_post_ref[...])


def _ab_pack_w_in(w_in):
    sizes = (M_QK, M_QK, M_V, M_V, M_HEADS, M_HEADS, G_QK, G_QK, G_V, G_V, G_HEADS, G_HEADS)
    offs = [0]
    for sz in sizes:
        offs.append(offs[-1] + sz)
    seg = [w_in[:, offs[i]:offs[i + 1]] for i in range(len(sizes))]
    m_q, m_k, m_v, m_o, m_i, m_f, g_q, g_k, g_v, g_gate, g_a, g_b = seg
    gates = jnp.concatenate([m_i, m_f, g_a, g_b], axis=1)
    gates = jnp.pad(gates, ((0, 0), (0, GATE_LANES - gates.shape[1])))
    return jnp.concatenate([m_q, m_k, g_q, g_k, g_v, m_v, m_o, g_gate, gates], axis=1).astype(BF16)


def _lane_vec(pieces):
    v = jnp.zeros((GATE_LANES,), F32)
    for off, vals in pieces:
        v = lax.dynamic_update_slice(v, vals.astype(F32), (off,))
    return v.reshape(1, GATE_LANES)


def _ab_mixer(x, g_pre, g_post, w_in, conv_m, m_gate_bias, m_norm, conv_g, g_a_log, g_dt_bias, g_norm, w_out,
              *, block=512):
    B, S, D = x.shape
    chunk = AB_CHUNK
    tb = min(block, S)
    n_chunks = tb // chunk
    group = min(AB_GROUP, n_chunks)
    assert S % tb == 0 and tb % chunk == 0 and n_chunks % group == 0
    tri3, masks = _ab_tables(chunk)
    gbias = _lane_vec([(LANE_I, m_gate_bias[0]), (LANE_F, m_gate_bias[1]), (LANE_A, g_dt_bias)])
    alog = _lane_vec([(LANE_A, g_a_log)])
    kern = functools.partial(_ab_kernel, chunk=chunk, group=group)
    mix = M_V + G_V
    return pl.pallas_call(
        kern,
        grid=(B, S // tb),
        in_specs=[
            pl.BlockSpec((1, tb, D), lambda b, s: (b, s, 0)),
            _resident((1, D)), _resident((1, D)),
            _resident((D, AB_COLS)), _resident(conv_m.shape), _resident(conv_g.shape),
            _resident((1, GATE_LANES)), _resident((1, GATE_LANES)),
            _resident((1, M_V)), _resident((1, G_V)), _resident((mix, D)),
            _resident(tri3.shape), _resident(masks.shape),
        ],
        out_specs=pl.BlockSpec((1, tb, D), lambda b, s: (b, s, 0)),
        out_shape=jax.ShapeDtypeStruct((B, S, D), F32),
        scratch_shapes=[
            pltpu.VMEM((tb, D), BF16),
            pltpu.VMEM((N_CONV_TILES, HALO + tb, LANES), F32),
            pltpu.VMEM((tb, PLAIN_COLS), F32),
            pltpu.VMEM((tb, ACT_COLS), F32),
            pltpu.VMEM((tb, mix), BF16),
            pltpu.VMEM((M_HEADS, tb, M_DK), BF16),
            pltpu.VMEM((M_HEADS, tb, M_DK), F32),
            pltpu.VMEM((M_HEADS, tb, chunk), F32),
            pltpu.VMEM((M_HEADS, tb, chunk), F32),
            pltpu.VMEM((M_HEADS * N_MVEC, tb, GATE_LANES), F32),
            pltpu.VMEM((n_chunks, M_HEADS * N_MSC, HALO, GATE_LANES), F32),
            pltpu.VMEM((G_HEADS, tb, G_DV), F32),
            pltpu.VMEM((G_HEADS, tb, G_DK), BF16),
            pltpu.VMEM((G_HEADS, tb, G_DK), BF16),
            pltpu.VMEM((G_HEADS, tb, G_DK), BF16),
            pltpu.VMEM((G_HEADS, tb, chunk), BF16),
            pltpu.VMEM((n_chunks, G_HEADS, HALO, GATE_LANES), F32),
            pltpu.VMEM((M_HEADS, M_DK, M_DV + 128), F32),
            pltpu.VMEM((M_HEADS, HALO, GATE_LANES), F32),
            pltpu.VMEM((G_HEADS, G_DK, G_DV), F32),
        ],
        compiler_params=pltpu.CompilerParams(
            dimension_semantics=("arbitrary", "arbitrary"), vmem_limit_bytes=VMEM_LIMIT_BYTES),
        name="ab_mixer",
    )(x, g_pre.reshape(1, D), g_post.reshape(1, D), _ab_pack_w_in(w_in), conv_m, conv_g, gbias, alog,
      m_norm.reshape(1, M_V), g_norm.reshape(1, G_V), w_out.astype(BF16),
      jnp.asarray(tri3, BF16), jnp.asarray(masks, F32))


def kernel(x, norm_gains, ab_w_in, ab_conv_m, ab_m_gate_bias, ab_m_norm, ab_conv_g, ab_g_a_log, ab_g_dt_bias, ab_g_norm, ab_w_out, c_w_in, c_lb_logits, c_norm, c_w_out, ffn_w_up, ffn_conv_w, ffn_conv_b, ffn_w_down):
    depth = norm_gains.shape[0]
    p = jax.nn.softmax(c_lb_logits.astype(F32), axis=0)
    lower_bounds = jnp.cumsum(p, axis=0) - p[0]
    c_heads = c_w_out.shape[1] // 128
    for layer in range(depth):
        j = layer // 2
        g = norm_gains[layer]
        if layer % 2 == 0:
            x = _ab_mixer(x, g[0], g[1], ab_w_in[j], ab_conv_m[j], ab_m_gate_bias[j], ab_m_norm[j],
                          ab_conv_g[j], ab_g_a_log[j], ab_g_dt_bias[j], ab_g_norm[j], ab_w_out[j])
        else:
            x = _hgrn2_mixer(x, g[0], g[1], c_w_in[j], lower_bounds[layer], c_norm[j], c_w_out[j],
                             heads=c_heads)
        x = _ffn(x, g[2], g[3], ffn_w_up[layer], ffn_conv_w[layer], ffn_conv_b[layer], ffn_w_down[layer])
    return x
```

```python
import functools

import jax
import jax.numpy as jnp
from jax import lax
from jax.experimental import pallas as pl
from jax.experimental.pallas import tpu as pltpu

F32 = jnp.float32
BF16 = jnp.bfloat16
EPS = 1e-6
LANES = 128
HALO = 8
FFN_DOWN_LAG = 2
FFN_DOWN_PART = 4
VMEM_LIMIT_BYTES = 56 * 1024 * 1024


def _rms(x, gain):
    return x * lax.rsqrt(jnp.mean(x * x, axis=-1, keepdims=True) + EPS) * gain


def _sigmoid(x):
    return 0.5 * jnp.tanh(0.5 * x) + 0.5


def _silu(x):
    return x * _sigmoid(x)


def _log1p_exp_neg_abs(x):
    return jnp.log(1.0 + jnp.exp(-jnp.abs(x)))


def _resident(shape):
    nd = len(shape)
    return pl.BlockSpec(shape, lambda b, s: (0,) * nd, pipeline_mode=pl.Buffered(1))


def _ffn_kernel(x_ref, g_pre_ref, g_post_ref, wup_ref, cw_ref, cb_ref, wdn_ref, o_ref,
                u_scr, a_scr, h_scr, carry_scr, *, d_ff, slab):
    tb = x_ref.shape[1]

    @pl.when(pl.program_id(1) == 0)
    def _():
        carry_scr[...] = jnp.zeros_like(carry_scr)

    u_scr[...] = _rms(x_ref[0], g_pre_ref[...]).astype(BF16)
    n_slab = d_ff // slab

    def conv_half(buf, c0):
        h = jnp.dot(u_scr[...], wup_ref[:, c0:c0 + slab], preferred_element_type=F32)
        outs = []
        for lt in range(slab // LANES):
            cols = slice(c0 + lt * LANES, c0 + (lt + 1) * LANES)
            hl = h[:, lt * LANES:(lt + 1) * LANES]
            h_scr[buf, lt, 0:HALO, :] = carry_scr[:, cols]
            h_scr[buf, lt, HALO:HALO + tb, :] = hl
            carry_scr[:, cols] = hl[tb - HALO:, :]
            w = cw_ref[:, cols]
            outs.append(w[0:1] * h_scr[buf, lt, HALO - 2:HALO - 2 + tb, :]
                        + w[1:2] * h_scr[buf, lt, HALO - 1:HALO - 1 + tb, :]
                        + w[2:3] * hl + cb_ref[:, cols])
        return outs

    out, done = None, 0
    for j in range(n_slab):
        gate = conv_half(2 * (j % 2), j * slab)
        val = conv_half(2 * (j % 2) + 1, d_ff + j * slab)
        for lt, (g, v) in enumerate(zip(gate, val)):
            a_scr[:, j * slab + lt * LANES:j * slab + (lt + 1) * LANES] = (_silu(g) * v).astype(BF16)
        ready = j + 1 - FFN_DOWN_LAG
        if ready - done >= FFN_DOWN_PART or (j == n_slab - 1 and ready > done):
            part = jnp.dot(a_scr[:, done * slab:ready * slab], wdn_ref[done * slab:ready * slab, :],
                           preferred_element_type=F32)
            out = part if out is None else out + part
            done = ready
    if done < n_slab:
        part = jnp.dot(a_scr[:, done * slab:], wdn_ref[done * slab:, :], preferred_element_type=F32)
        out = part if out is None else out + part
    o_ref[0] = x_ref[0] + _rms(out, g_post_ref[...])


def _ffn(x, g_pre, g_post, w_up, conv_w, conv_b, w_down, *, block=1024, slab=256):
    B, S, D = x.shape
    d_ff = w_down.shape[0]
    tb = min(block, S)
    assert S % tb == 0 and d_ff % slab == 0
    kern = functools.partial(_ffn_kernel, d_ff=d_ff, slab=slab)
    return pl.pallas_call(
        kern,
        grid=(B, S // tb),
        in_specs=[
            pl.BlockSpec((1, tb, D), lambda b, s: (b, s, 0)),
            _resident((1, D)), _resident((1, D)),
            _resident((D, 2 * d_ff)), _resident((3, 2 * d_ff)), _resident((1, 2 * d_ff)),
            _resident((d_ff, D)),
        ],
        out_specs=pl.BlockSpec((1, tb, D), lambda b, s: (b, s, 0)),
        out_shape=jax.ShapeDtypeStruct((B, S, D), F32),
        scratch_shapes=[
            pltpu.VMEM((tb, D), BF16),
            pltpu.VMEM((tb, d_ff), BF16),
            pltpu.VMEM((4, slab // LANES, HALO + tb, LANES), F32),
            pltpu.VMEM((HALO, 2 * d_ff), F32),
        ],
        compiler_params=pltpu.CompilerParams(
            dimension_semantics=("arbitrary", "arbitrary"), vmem_limit_bytes=VMEM_LIMIT_BYTES),
        name="ffn",
    )(x, g_pre.reshape(1, D), g_post.reshape(1, D), w_up.astype(BF16), conv_w,
      conv_b.reshape(1, 2 * d_ff), w_down.astype(BF16))


def _dot(a, b):
    return jnp.dot(a, b, preferred_element_type=F32)


def _dot_nt(a, b):
    return lax.dot_general(a, b, (((1,), (1,)), ((), ())), preferred_element_type=F32)


def _dot_tn(a, b):
    return lax.dot_general(a, b, (((0,), (0,)), ((), ())), preferred_element_type=F32)


def _bf(x):
    return x.astype(BF16)


def _split3(x):
    hi = x.astype(BF16)
    r1 = x - hi.astype(F32)
    mid = r1.astype(BF16)
    lo = (r1 - mid.astype(F32)).astype(BF16)
    return jnp.concatenate([hi, mid, lo], axis=0)


def _log_sigmoid(x):
    return jnp.minimum(x, 0.0) - _log1p_exp_neg_abs(x)


def _level_sizes(chunk):
    sizes, m = [], chunk // 2
    while m >= 1:
        sizes.append(m)
        m //= 2
    return sizes


def _hgrn2_tables(chunk):
    import numpy as np
    L = chunk
    idx = np.arange(L)
    mats = [(idx[None, :] <= idx[:, None]).astype(np.float32)]
    masks = []
    for m in _level_sizes(L):
        in_b = (idx % (2 * m)) >= m
        blk_start = (idx // m) * m
        blk_end = blk_start + m - 1
        c = idx[None, :]
        mat_b = (c >= blk_start[:, None]) & (c <= idx[:, None])
        mat_a = (c > idx[:, None]) & (c <= blk_end[:, None])
        mats.append(np.where(in_b[:, None], mat_b, mat_a).astype(np.float32))
        same_pair = (idx[:, None] // (2 * m)) == (idx[None, :] // (2 * m))
        masks.append((same_pair & in_b[:, None] & (~in_b)[None, :]).astype(np.float32))
    masks.append(np.eye(L, dtype=np.float32))
    mstack = np.concatenate(mats, axis=0)
    return np.concatenate([mstack] * 3, axis=1), np.stack(masks)


HGRN2_HEAD_GROUP = 4
HGRN2_UNROLL = 2


def _hgrn2_kernel(x_ref, g_pre_ref, g_post_ref, win_ref, lb_ref, ng_ref, wout_ref, mstack_ref, mask_ref,
                  o_ref, u_scr, proj_scr, y_scr, q_scr, k_scr, xall_scr, st_scr, *, heads, dk, chunk, unroll):
    tb = x_ref.shape[1]
    hk = heads * dk
    n_lvl = len(_level_sizes(chunk))

    @pl.when(pl.program_id(1) == 0)
    def _():
        st_scr[...] = jnp.zeros_like(st_scr)

    u_scr[...] = _rms(x_ref[0], g_pre_ref[...]).astype(BF16)
    for j in range(4):
        proj_scr[:, j * hk:(j + 1) * hk] = _dot(u_scr[...], win_ref[:, j * hk:(j + 1) * hk])

    lb = lb_ref[...]
    log_lb = jnp.log(lb)
    log1m_lb = jnp.log1p(-lb)

    def decay_phase(c, buf):
        rows = pl.ds(pl.multiple_of(c * chunk, chunk), chunk)
        ff = proj_scr[rows, hk:2 * hk]
        q_scr[buf] = _bf(_silu(proj_scr[rows, 0:hk]) * (dk ** -0.5))
        k_scr[buf] = _bf((1.0 - lb) * _sigmoid(-ff))
        a = log1m_lb + _log_sigmoid(ff)
        log_f = jnp.maximum(log_lb, a) + _log1p_exp_neg_abs(log_lb - a)
        xall_scr[buf] = _dot(mstack_ref[...], _split3(log_f))

    def heads_phase(c, buf):
        rows = pl.ds(pl.multiple_of(c * chunk, chunk), chunk)
        for h0 in range(0, heads, HGRN2_HEAD_GROUP):
            hs = range(h0, min(h0 + HGRN2_HEAD_GROUP, heads))
            cols = {h: slice(h * dk, (h + 1) * dk) for h in hs}
            attn = {}
            for lvl in range(n_lvl + 1):
                for h in hs:
                    q = q_scr[buf, :, cols[h]]
                    k = k_scr[buf, :, cols[h]]
                    if lvl < n_lvl:
                        e = _bf(jnp.exp(xall_scr[buf, (lvl + 1) * chunk:(lvl + 2) * chunk, cols[h]]))
                        q, k = q * e, k * e
                    prod = _dot_nt(q, k) * mask_ref[lvl]
                    attn[h] = prod if lvl == 0 else attn[h] + prod
            g = {h: xall_scr[buf, 0:chunk, cols[h]] for h in hs}
            v = {h: proj_scr[rows, 2 * hk + h * dk:2 * hk + (h + 1) * dk].astype(BF16) for h in hs}
            st = {h: st_scr[h] for h in hs}
            o_inter = {h: _dot_nt(q_scr[buf, :, cols[h]] * _bf(jnp.exp(g[h])), st[h].astype(BF16))
                       for h in hs}
            kv = {}
            for h in hs:
                g_last = g[h][chunk - 1:chunk, :]
                k_dec = k_scr[buf, :, cols[h]] * _bf(jnp.exp(g_last - g[h]))
                kv[h] = _dot_tn(v[h], k_dec)
            o_intra = {h: _dot(attn[h].astype(BF16), v[h]) for h in hs}
            for h in hs:
                st_scr[h] = st[h] * jnp.exp(g[h][chunk - 1:chunk, :]) + kv[h]
                o = o_inter[h] + o_intra[h]
                gate = proj_scr[rows, 3 * hk + h * dk:3 * hk + (h + 1) * dk]
                y = (o * lax.rsqrt(jnp.mean(o * o, axis=-1, keepdims=True) + EPS) * ng_ref[:, cols[h]]
                     * _silu(gate))
                y_scr[rows, cols[h]] = y.astype(BF16)

    def chunks_body(i, carry):
        for j in range(unroll):
            decay_phase(i * unroll + j, j)
        for j in range(unroll):
            heads_phase(i * unroll + j, j)
        return carry

    lax.fori_loop(0, tb // chunk // unroll, chunks_body, 0)
    out = _dot(y_scr[...], wout_ref[...])
    o_ref[0] = x_ref[0] + _rms(out, g_post_ref[...])


def _hgrn2_mixer(x, g_pre, g_post, w_in, lower_bound, norm_gain, w_out, *, heads, block=1024, chunk=64):
    B, S, D = x.shape
    hk = w_out.shape[0]
    dk = hk // heads
    tb = min(block, S)
    unroll = min(HGRN2_UNROLL, tb // chunk)
    assert S % tb == 0 and tb % (chunk * unroll) == 0 and w_in.shape[1] == 4 * hk
    mstack, masks = _hgrn2_tables(chunk)
    n_rows = mstack.shape[0]
    kern = functools.partial(_hgrn2_kernel, heads=heads, dk=dk, chunk=chunk, unroll=unroll)
    return pl.pallas_call(
        kern,
        grid=(B, S // tb),
        in_specs=[
            pl.BlockSpec((1, tb, D), lambda b, s: (b, s, 0)),
            _resident((1, D)), _resident((1, D)),
            _resident((D, 4 * hk)), _resident((1, hk)), _resident((1, hk)), _resident((hk, D)),
            _resident(mstack.shape), _resident(masks.shape),
        ],
        out_specs=pl.BlockSpec((1, tb, D), lambda b, s: (b, s, 0)),
        out_shape=jax.ShapeDtypeStruct((B, S, D), F32),
        scratch_shapes=[
            pltpu.VMEM((tb, D), BF16),
            pltpu.VMEM((tb, 4 * hk), F32),
            pltpu.VMEM((tb, hk), BF16),
            pltpu.VMEM((unroll, chunk, hk), BF16),
            pltpu.VMEM((unroll, chunk, hk), BF16),
            pltpu.VMEM((unroll, n_rows, hk), F32),
            pltpu.VMEM((heads, dk, dk), F32),
        ],
        compiler_params=pltpu.CompilerParams(
            dimension_semantics=("arbitrary", "arbitrary"), vmem_limit_bytes=VMEM_LIMIT_BYTES),
        name="hgrn2_mixer",
    )(x, g_pre.reshape(1, D), g_post.reshape(1, D), w_in.astype(BF16), lower_bound.reshape(1, hk),
      norm_gain.reshape(1, hk), w_out.astype(BF16), jnp.asarray(mstack, BF16), jnp.asarray(masks, F32))


M_HEADS, M_DK, M_DV = 4, 64, 128
G_HEADS, G_DK, G_DV = 4, 128, 128
M_QK = M_HEADS * M_DK
M_V = M_HEADS * M_DV
G_QK = G_HEADS * G_DK
G_V = G_HEADS * G_DV
AB_CHUNK = 64
AB_GROUP = 4
CONV_K = 4
GATE_SOFTCAP = 15.0
GATE_LANES = 128
CT_MQ = 0
CT_MK = CT_MQ + M_QK // LANES
CT_GQ = CT_MK + M_QK // LANES
CT_GK = CT_GQ + G_QK // LANES
CT_GV = CT_GK + G_QK // LANES
N_CONV_TILES = CT_GV + G_V // LANES
CONV_COLS = N_CONV_TILES * LANES
P_MV = 0
P_MO = P_MV + M_V
P_GG = P_MO + M_V
P_GATES = P_GG + G_V
PLAIN_COLS = P_GATES + GATE_LANES
AB_COLS = CONV_COLS + PLAIN_COLS
LANE_I, LANE_F, LANE_A, LANE_B = 0, M_HEADS, 2 * M_HEADS, 2 * M_HEADS + G_HEADS
ACT_MQ, ACT_MK = 0, M_QK
ACT_GQ = 2 * M_QK
ACT_GK = ACT_GQ + G_QK
ACT_GV = ACT_GK + G_QK
ACT_COLS = ACT_GV + G_V
N_MVEC = 3
N_MSC = 2


def _ab_tables(chunk):
    import numpy as np
    idx = np.arange(chunk)
    l, s = idx[:, None], idx[None, :]
    incl = l >= s
    strict = l > s
    blk16 = (l // 16) == (s // 16)
    blk32 = (l // 32) == (s // 32)
    masks = np.stack([incl, strict, blk16, blk32 & ~blk16, ~blk32, l == s]).astype(np.float32)
    tri3 = np.concatenate([incl.astype(np.float32)] * 3, axis=1)
    return tri3, masks


def _bcast_col(a, lane_ids, j):
    col = jnp.sum(jnp.where(lane_ids == j, a, 0.0), axis=-1, keepdims=True)
    return jnp.broadcast_to(col, a.shape)


def _unit_lower_inverses(ns, blk16, off32, off64, eye):
    d = [n * blk16 for n in ns]
    db = [_bf(x) for x in d]
    d2 = [_dot(x, x) for x in db]
    d2b = [_bf(x) for x in d2]
    p = [_dot(_bf(eye - a), _bf(eye + b)) for a, b in zip(d, d2)]
    d4 = [_dot(x, x) for x in d2b]
    d4b = [_bf(x) for x in d4]
    p = [_dot(_bf(a), _bf(eye + b)) for a, b in zip(p, d4)]
    d8 = [_dot(x, x) for x in d4b]
    p = [_dot(_bf(a), _bf(eye + b)) for a, b in zip(p, d8)]
    for off in (off32, off64):
        pb = [_bf(x) for x in p]
        t = [_dot(a, _bf(n * off)) for a, n in zip(pb, ns)]
        t = [_dot(_bf(a), b) for a, b in zip(t, pb)]
        p = [a - b for a, b in zip(p, t)]
    return p


def _ab_kernel(x_ref, g_pre_ref, g_post_ref, win_ref, cm_ref, cg_ref, gbias_ref, alog_ref, mn_ref, gn_ref,
               wout_ref, tri3_ref, mask_ref, o_ref,
               u_scr, conv_scr, proj_scr, act_scr, y_scr,
               mq_scr, mk_scr, mqk_scr, mlogd_scr, mvec_scr, msc_scr,
               gu_scr, gw_scr, gqd_scr, gkd_scr, gattn_scr, ggl_scr,
               cn_scr, m_scr, s_scr, *, chunk, group):
    tb = x_ref.shape[1]
    n_chunks = tb // chunk

    @pl.when(pl.program_id(1) == 0)
    def _():
        conv_scr[:, 0:HALO, :] = jnp.zeros((N_CONV_TILES, HALO, LANES), F32)
        cn_scr[...] = jnp.zeros_like(cn_scr)
        m_scr[...] = jnp.zeros_like(m_scr)
        s_scr[...] = jnp.zeros_like(s_scr)

    u_scr[...] = _bf(_rms(x_ref[0], g_pre_ref[...]))
    for c0 in range(0, CONV_COLS, 512):
        r = _dot(u_scr[...], win_ref[:, c0:c0 + 512])
        for lt in range(512 // LANES):
            conv_scr[c0 // LANES + lt, HALO:HALO + tb, :] = r[:, lt * LANES:(lt + 1) * LANES]
    for c0 in range(0, PLAIN_COLS, 512):
        c1 = min(c0 + 512, PLAIN_COLS)
        proj_scr[:, c0:c1] = _dot(u_scr[...], win_ref[:, CONV_COLS + c0:CONV_COLS + c1])

    def conv_silu(ci, lt, w_ref, wcol0):
        r0 = HALO + ci * chunk - (CONV_K - 1)
        acc = None
        for j in range(CONV_K):
            t = w_ref[j:j + 1, wcol0:wcol0 + LANES] * conv_scr[lt, r0 + j:r0 + j + chunk, :]
            acc = t if acc is None else acc + t
        return _silu(acc)

    def unit(v):
        return v * lax.rsqrt(jnp.sum(v * v, axis=-1, keepdims=True) + EPS)

    for ci in range(n_chunks):
        arow = slice(ci * chunk, (ci + 1) * chunk)
        for t in range(M_QK // LANES):
            mq = conv_silu(ci, CT_MQ + t, cm_ref, t * LANES)
            act_scr[arow, ACT_MQ + t * LANES:ACT_MQ + (t + 1) * LANES] = mq * (M_DK ** -0.5)
            mk = conv_silu(ci, CT_MK + t, cm_ref, M_QK + t * LANES)
            act_scr[arow, ACT_MK + t * LANES:ACT_MK + (t + 1) * LANES] = mk
        for h in range(G_HEADS):
            gq = conv_silu(ci, CT_GQ + h, cg_ref, h * G_DK)
            act_scr[arow, ACT_GQ + h * G_DK:ACT_GQ + (h + 1) * G_DK] = unit(gq) * (G_DK ** -0.5)
            gk = conv_silu(ci, CT_GK + h, cg_ref, G_QK + h * G_DK)
            act_scr[arow, ACT_GK + h * G_DK:ACT_GK + (h + 1) * G_DK] = unit(gk)
            gv = conv_silu(ci, CT_GV + h, cg_ref, 2 * G_QK + h * G_DV)
            act_scr[arow, ACT_GV + h * G_DV:ACT_GV + (h + 1) * G_DV] = gv
    conv_scr[:, 0:HALO, :] = conv_scr[:, tb:tb + HALO, :]

    lane = lax.broadcasted_iota(jnp.int32, (chunk, GATE_LANES), 1)
    gbias = gbias_ref[...]
    neg_a = -jnp.exp(alog_ref[...])

    def prep_body(g, carry):
        m_incl, m_strict, m_blk16, m_off32, m_off64, m_eye = (mask_ref[i] for i in range(6))
        ns, betas, gbs, ks, vs, arows = [], [], [], [], [], []
        for i in range(group):
            c = g * group + i
            a0 = pl.multiple_of(c * chunk, chunk)
            arow = pl.ds(a0, chunk)

            z = proj_scr[arow, P_GATES:P_GATES + GATE_LANES] + gbias
            capped = GATE_SOFTCAP * jnp.tanh(z / GATE_SOFTCAP)
            softplus = jnp.maximum(z, 0.0) + _log1p_exp_neg_abs(z)
            gp = jnp.where(lane < LANE_F, capped,
                           jnp.where(lane < LANE_A, _log_sigmoid(capped),
                                     jnp.where(lane < LANE_B, neg_a * softplus, _sigmoid(z))))
            cs = _dot(tri3_ref[...], _split3(gp))
            gp_t = gp.T
            cs_t = cs.T

            for h in range(M_HEADS):
                q = _bf(act_scr[arow, ACT_MQ + h * M_DK:ACT_MQ + (h + 1) * M_DK])
                k = act_scr[arow, ACT_MK + h * M_DK:ACT_MK + (h + 1) * M_DK]
                b_b = _bcast_col(cs, lane, LANE_F + h)
                ig_b = _bcast_col(gp, lane, LANE_I + h)
                b_row = cs_t[LANE_F + h:LANE_F + h + 1, :]
                ig_row = gp_t[LANE_I + h:LANE_I + h + 1, :]
                log_d = jnp.where(m_incl > 0.0, b_b[:, 0:chunk] - b_row + ig_row, -jnp.inf)
                b_last = b_b[chunk - 1:chunk, :]
                log_w = b_last - b_b + ig_b
                mq_scr[h, arow, :] = q
                mk_scr[h, arow, :] = k
                mqk_scr[h, arow, :] = _dot_nt(q, _bf(k))
                mlogd_scr[h, arow, :] = log_d
                mvec_scr[h * N_MVEC + 0, arow, :] = b_b
                mvec_scr[h * N_MVEC + 1, arow, :] = jnp.broadcast_to(
                    jnp.max(log_d, axis=-1, keepdims=True), (chunk, GATE_LANES))
                mvec_scr[h * N_MVEC + 2, arow, :] = log_w
                msc_scr[c, h * N_MSC + 0] = jnp.broadcast_to(b_last, (HALO, GATE_LANES))
                msc_scr[c, h * N_MSC + 1] = jnp.broadcast_to(
                    jnp.max(log_w, axis=0, keepdims=True), (HALO, GATE_LANES))

            for h in range(G_HEADS):
                q = act_scr[arow, ACT_GQ + h * G_DK:ACT_GQ + (h + 1) * G_DK]
                k = act_scr[arow, ACT_GK + h * G_DK:ACT_GK + (h + 1) * G_DK]
                v = act_scr[arow, ACT_GV + h * G_DV:ACT_GV + (h + 1) * G_DV]
                g_b = _bcast_col(cs, lane, LANE_A + h)
                beta = _bcast_col(gp, lane, LANE_B + h)
                g_row = cs_t[LANE_A + h:LANE_A + h + 1, :]
                decay = jnp.where(m_incl > 0.0,
                                  jnp.exp(jnp.where(m_incl > 0.0, g_b[:, 0:chunk] - g_row, 0.0)), 0.0)
                kb = _bf(k)
                ns.append(m_strict * (beta[:, 0:chunk] * _dot_nt(kb, kb) * decay))
                gattn_scr[h, arow, :] = _bf(_dot_nt(_bf(q), kb) * decay)
                g_last = g_b[chunk - 1:chunk, :]
                gqd_scr[h, arow, :] = _bf(q * jnp.exp(g_b))
                gkd_scr[h, arow, :] = _bf(k * jnp.exp(g_last - g_b))
                ggl_scr[c, h] = jnp.broadcast_to(jnp.exp(g_last), (HALO, GATE_LANES))
                betas.append(beta)
                gbs.append(g_b)
                ks.append(k)
                vs.append(v)
                arows.append((h, arow))

        t_inv = _unit_lower_inverses(ns, m_blk16, m_off32, m_off64, m_eye)
        for t, beta, g_b, k, v, (h, arow) in zip(t_inv, betas, gbs, ks, vs, arows):
            rhs = jnp.concatenate([v * beta, k * (beta * jnp.exp(g_b))], axis=1)
            sol = _dot(_bf(t), _bf(rhs))
            gu_scr[h, arow, :] = sol[:, 0:G_DV]
            gw_scr[h, arow, :] = _bf(sol[:, G_DV:G_DV + G_DK])
        return carry

    lax.fori_loop(0, n_chunks // group, prep_body, 0)

    ones_lane0 = _bf(jnp.where(lax.broadcasted_iota(jnp.int32, (chunk, M_DV), 1) == 0, 1.0, 0.0))

    def seq_body(c, carry):
        a0 = pl.multiple_of(c * chunk, chunk)
        arow = pl.ds(a0, chunk)

        m_t, w_inter, s, v_ext, kw, c_decay, m_new = [], [], [], [], [], [], []
        for h in range(M_HEADS):
            m_prev = m_scr[h, 0:1, :]
            b_b = mvec_scr[h * N_MVEC + 0, arow, :]
            b_last = msc_scr[c, h * N_MSC + 0, 0:1, :]
            log_inter = b_b + m_prev
            mt = jnp.maximum(mvec_scr[h * N_MVEC + 1, arow, :], log_inter)
            mn = jnp.maximum(b_last + m_prev, msc_scr[c, h * N_MSC + 1, 0:1, :])
            wk = jnp.exp(mvec_scr[h * N_MVEC + 2, arow, :] - mn)
            m_t.append(mt)
            w_inter.append(jnp.exp(log_inter - mt))
            s.append(_bf(mqk_scr[h, arow, :] * jnp.exp(mlogd_scr[h, arow, :] - mt[:, 0:chunk])))
            v = _bf(proj_scr[arow, P_MV + h * M_DV:P_MV + (h + 1) * M_DV])
            v_ext.append(jnp.concatenate([v, ones_lane0], axis=1))
            kw.append(_bf(mk_scr[h, arow, :] * wk[:, 0:M_DK]))
            c_decay.append(jnp.exp(b_last + m_prev - mn))
            m_new.append(mn)

        cn = [cn_scr[h] for h in range(M_HEADS)]
        st = [s_scr[h] for h in range(G_HEADS)]
        wq = [_dot(jnp.concatenate([gw_scr[h, arow, :], gqd_scr[h, arow, :]], axis=0), _bf(st[h]))
              for h in range(G_HEADS)]
        sv = [_dot(s[h], v_ext[h]) for h in range(M_HEADS)]
        qc = [_dot(mq_scr[h, arow, :], _bf(cn[h])) for h in range(M_HEADS)]
        upd = [_dot_tn(kw[h], v_ext[h]) for h in range(M_HEADS)]
        v_new = [_bf(gu_scr[h, arow, :] - wq[h][0:chunk]) for h in range(G_HEADS)]
        o_g = [wq[h][chunk:2 * chunk] + _dot(gattn_scr[h, arow, :], v_new[h]) for h in range(G_HEADS)]
        s_upd = [_dot_tn(gkd_scr[h, arow, :], v_new[h]) for h in range(G_HEADS)]

        for h in range(M_HEADS):
            wi = w_inter[h]
            nd = sv[h] + jnp.concatenate([wi, wi], axis=1) * qc[h]
            den = jnp.broadcast_to(nd[:, M_DV:M_DV + 1], (chunk, M_DV))
            hm = nd[:, 0:M_DV] / jnp.maximum(jnp.abs(den), jnp.exp(-m_t[h]))
            cd = c_decay[h]
            cn_scr[h] = jnp.concatenate([cd, cd], axis=1) * cn[h] + upd[h]
            m_scr[h] = jnp.broadcast_to(m_new[h], (HALO, GATE_LANES))
            gate = proj_scr[arow, P_MO + h * M_DV:P_MO + (h + 1) * M_DV]
            hn = hm * lax.rsqrt(jnp.mean(hm * hm, axis=-1, keepdims=True) + EPS)
            y_scr[arow, h * M_DV:(h + 1) * M_DV] = _bf(
                _sigmoid(gate) * hn * mn_ref[:, h * M_DV:(h + 1) * M_DV])

        for h in range(G_HEADS):
            s_scr[h] = ggl_scr[c, h, 0:1, :] * st[h] + s_upd[h]
            o = o_g[h]
            gate = proj_scr[arow, P_GG + h * G_DV:P_GG + (h + 1) * G_DV]
            on = o * lax.rsqrt(jnp.mean(o * o, axis=-1, keepdims=True) + EPS)
            y_scr[arow, M_V + h * G_DV:M_V + (h + 1) * G_DV] = _bf(
                on * gn_ref[:, h * G_DV:(h + 1) * G_DV] * _silu(gate))
        return carry

    lax.fori_loop(0, n_chunks, seq_body, 0)
    out = _dot(y_scr[...], wout_ref[...])
    o_ref[0] = x_ref[0] + _rms(out, g_post_ref[...])


def _ab_pack_w_in(w_in):
    sizes = (M_QK, M_QK, M_V, M_V, M_HEADS, M_HEADS, G_QK, G_QK, G_V, G_V, G_HEADS, G_HEADS)
    offs = [0]
    for sz in sizes:
        offs.append(offs[-1] + sz)
    seg = [w_in[:, offs[i]:offs[i + 1]] for i in range(len(sizes))]
    m_q, m_k, m_v, m_o, m_i, m_f, g_q, g_k, g_v, g_gate, g_a, g_b = seg
    gates = jnp.concatenate([m_i, m_f, g_a, g_b], axis=1)
    gates = jnp.pad(gates, ((0, 0), (0, GATE_LANES - gates.shape[1])))
    return jnp.concatenate([m_q, m_k, g_q, g_k, g_v, m_v, m_o, g_gate, gates], axis=1).astype(BF16)


def _lane_vec(pieces):
    v = jnp.zeros((GATE_LANES,), F32)
    for off, vals in pieces:
        v = lax.dynamic_update_slice(v, vals.astype(F32), (off,))
    return v.reshape(1, GATE_LANES)


def _ab_mixer(x, g_pre, g_post, w_in, conv_m, m_gate_bias, m_norm, conv_g, g_a_log, g_dt_bias, g_norm, w_out,
              *, block=512):
    B, S, D = x.shape
    chunk = AB_CHUNK
    tb = min(block, S)
    n_chunks = tb // chunk
    group = min(AB_GROUP, n_chunks)
    assert S % tb == 0 and tb % chunk == 0 and n_chunks % group == 0
    tri3, masks = _ab_tables(chunk)
    gbias = _lane_vec([(LANE_I, m_gate_bias[0]), (LANE_F, m_gate_bias[1]), (LANE_A, g_dt_bias)])
    alog = _lane_vec([(LANE_A, g_a_log)])
    kern = functools.partial(_ab_kernel, chunk=chunk, group=group)
    mix = M_V + G_V
    return pl.pallas_call(
        kern,
        grid=(B, S // tb),
        in_specs=[
            pl.BlockSpec((1, tb, D), lambda b, s: (b, s, 0)),
            _resident((1, D)), _resident((1, D)),
            _resident((D, AB_COLS)), _resident(conv_m.shape), _resident(conv_g.shape),
            _resident((1, GATE_LANES)), _resident((1, GATE_LANES)),
            _resident((1, M_V)), _resident((1, G_V)), _resident((mix, D)),
            _resident(tri3.shape), _resident(masks.shape),
        ],
        out_specs=pl.BlockSpec((1, tb, D), lambda b, s: (b, s, 0)),
        out_shape=jax.ShapeDtypeStruct((B, S, D), F32),
        scratch_shapes=[
            pltpu.VMEM((tb, D), BF16),
            pltpu.VMEM((N_CONV_TILES, HALO + tb, LANES), F32),
            pltpu.VMEM((tb, PLAIN_COLS), F32),
            pltpu.VMEM((tb, ACT_COLS), F32),
            pltpu.VMEM((tb, mix), BF16),
            pltpu.VMEM((M_HEADS, tb, M_DK), BF16),
            pltpu.VMEM((M_HEADS, tb, M_DK), F32),
            pltpu.VMEM((M_HEADS, tb, chunk), F32),
            pltpu.VMEM((M_HEADS, tb, chunk), F32),
            pltpu.VMEM((M_HEADS * N_MVEC, tb, GATE_LANES), F32),
            pltpu.VMEM((n_chunks, M_HEADS * N_MSC, HALO, GATE_LANES), F32),
            pltpu.VMEM((G_HEADS, tb, G_DV), F32),
            pltpu.VMEM((G_HEADS, tb, G_DK), BF16),
            pltpu.VMEM((G_HEADS, tb, G_DK), BF16),
            pltpu.VMEM((G_HEADS, tb, G_DK), BF16),
            pltpu.VMEM((G_HEADS, tb, chunk), BF16),
            pltpu.VMEM((n_chunks, G_HEADS, HALO, GATE_LANES), F32),
            pltpu.VMEM((M_HEADS, M_DK, M_DV + 128), F32),
            pltpu.VMEM((M_HEADS, HALO, GATE_LANES), F32),
            pltpu.VMEM((G_HEADS, G_DK, G_DV), F32),
        ],
        compiler_params=pltpu.CompilerParams(
            dimension_semantics=("arbitrary", "arbitrary"), vmem_limit_bytes=VMEM_LIMIT_BYTES),
        name="ab_mixer",
    )(x, g_pre.reshape(1, D), g_post.reshape(1, D), _ab_pack_w_in(w_in), conv_m, conv_g, gbias, alog,
      m_norm.reshape(1, M_V), g_norm.reshape(1, G_V), w_out.astype(BF16),
      jnp.asarray(tri3, BF16), jnp.asarray(masks, F32))


def kernel(x, norm_gains, ab_w_in, ab_conv_m, ab_m_gate_bias, ab_m_norm, ab_conv_g, ab_g_a_log, ab_g_dt_bias, ab_g_norm, ab_w_out, c_w_in, c_lb_logits, c_norm, c_w_out, ffn_w_up, ffn_conv_w, ffn_conv_b, ffn_w_down):
    depth = norm_gains.shape[0]
    p = jax.nn.softmax(c_lb_logits.astype(F32), axis=0)
    lower_bounds = jnp.cumsum(p, axis=0) - p[0]
    c_heads = c_w_out.shape[1] // 128
    for layer in range(depth):
        j = layer // 2
        g = norm_gains[layer]
        if layer % 2 == 0:
            x = _ab_mixer(x, g[0], g[1], ab_w_in[j], ab_conv_m[j], ab_m_gate_bias[j], ab_m_norm[j],
                          ab_conv_g[j], ab_g_a_log[j], ab_g_dt_bias[j], ab_g_norm[j], ab_w_out[j])
        else:
            x = _hgrn2_mixer(x, g[0], g[1], c_w_in[j], lower_bounds[layer], c_norm[j], c_w_out[j],
                             heads=c_heads)
        x = _ffn(x, g[2], g[3], ffn_w_up[layer], ffn_conv_w[layer], ffn_conv_b[layer], ffn_w_down[layer])
    return x
```

```python
import functools

import jax
import jax.numpy as jnp
from jax import lax
from jax.experimental import pallas as pl
from jax.experimental.pallas import tpu as pltpu

F32 = jnp.float32
BF16 = jnp.bfloat16
EPS = 1e-6
LANES = 128
HALO = 8
FFN_DOWN_LAG = 2
FFN_DOWN_PART = 4
VMEM_LIMIT_BYTES = 56 * 1024 * 1024


def _rms(x, gain):
    return x * lax.rsqrt(jnp.mean(x * x, axis=-1, keepdims=True) + EPS) * gain


def _sigmoid(x):
    return 0.5 * jnp.tanh(0.5 * x) + 0.5


def _silu(x):
    return x * _sigmoid(x)


def _log1p_exp_neg_abs(x):
    return jnp.log(1.0 + jnp.exp(-jnp.abs(x)))


def _resident(shape):
    nd = len(shape)
    return pl.BlockSpec(shape, lambda b, s: (0,) * nd, pipeline_mode=pl.Buffered(1))


def _ffn_kernel(x_ref, g_pre_ref, g_post_ref, wup_ref, cw_ref, cb_ref, wdn_ref, o_ref,
                u_scr, a_scr, h_scr, carry_scr, *, d_ff, slab):
    tb = x_ref.shape[1]

    @pl.when(pl.program_id(1) == 0)
    def _():
        carry_scr[...] = jnp.zeros_like(carry_scr)

    u_scr[...] = _rms(x_ref[0], g_pre_ref[...]).astype(BF16)
    n_slab = d_ff // slab

    def conv_half(buf, c0):
        h = jnp.dot(u_scr[...], wup_ref[:, c0:c0 + slab], preferred_element_type=F32)
        outs = []
        for lt in range(slab // LANES):
            cols = slice(c0 + lt * LANES, c0 + (lt + 1) * LANES)
            hl = h[:, lt * LANES:(lt + 1) * LANES]
            h_scr[buf, lt, 0:HALO, :] = carry_scr[:, cols]
            h_scr[buf, lt, HALO:HALO + tb, :] = hl
            carry_scr[:, cols] = hl[tb - HALO:, :]
            w = cw_ref[:, cols]
            outs.append(w[0:1] * h_scr[buf, lt, HALO - 2:HALO - 2 + tb, :]
                        + w[1:2] * h_scr[buf, lt, HALO - 1:HALO - 1 + tb, :]
                        + w[2:3] * hl + cb_ref[:, cols])
        return outs

    out, done = None, 0
    for j in range(n_slab):
        gate = conv_half(2 * (j % 2), j * slab)
        val = conv_half(2 * (j % 2) + 1, d_ff + j * slab)
        for lt, (g, v) in enumerate(zip(gate, val)):
            a_scr[:, j * slab + lt * LANES:j * slab + (lt + 1) * LANES] = (_silu(g) * v).astype(BF16)
        ready = j + 1 - FFN_DOWN_LAG
        if ready - done >= FFN_DOWN_PART or (j == n_slab - 1 and ready > done):
            part = jnp.dot(a_scr[:, done * slab:ready * slab], wdn_ref[done * slab:ready * slab, :],
                           preferred_element_type=F32)
            out = part if out is None else out + part
            done = ready
    if done < n_slab:
        part = jnp.dot(a_scr[:, done * slab:], wdn_ref[done * slab:, :], preferred_element_type=F32)
        out = part if out is None else out + part
    o_ref[0] = x_ref[0] + _rms(out, g_post_ref[...])


def _ffn(x, g_pre, g_post, w_up, conv_w, conv_b, w_down, *, block=1024, slab=256):
    B, S, D = x.shape
    d_ff = w_down.shape[0]
    tb = min(block, S)
    assert S % tb == 0 and d_ff % slab == 0
    kern = functools.partial(_ffn_kernel, d_ff=d_ff, slab=slab)
    return pl.pallas_call(
        kern,
        grid=(B, S // tb),
        in_specs=[
            pl.BlockSpec((1, tb, D), lambda b, s: (b, s, 0)),
            _resident((1, D)), _resident((1, D)),
            _resident((D, 2 * d_ff)), _resident((3, 2 * d_ff)), _resident((1, 2 * d_ff)),
            _resident((d_ff, D)),
        ],
        out_specs=pl.BlockSpec((1, tb, D), lambda b, s: (b, s, 0)),
        out_shape=jax.ShapeDtypeStruct((B, S, D), F32),
        scratch_shapes=[
            pltpu.VMEM((tb, D), BF16),
            pltpu.VMEM((tb, d_ff), BF16),
            pltpu.VMEM((4, slab // LANES, HALO + tb, LANES), F32),
            pltpu.VMEM((HALO, 2 * d_ff), F32),
        ],
        compiler_params=pltpu.CompilerParams(
            dimension_semantics=("arbitrary", "arbitrary"), vmem_limit_bytes=VMEM_LIMIT_BYTES),
        name="ffn",
    )(x, g_pre.reshape(1, D), g_post.reshape(1, D), w_up.astype(BF16), conv_w,
      conv_b.reshape(1, 2 * d_ff), w_down.astype(BF16))


def _dot(a, b):
    return jnp.dot(a, b, preferred_element_type=F32)


def _dot_nt(a, b):
    return lax.dot_general(a, b, (((1,), (1,)), ((), ())), preferred_element_type=F32)


def _dot_tn(a, b):
    return lax.dot_general(a, b, (((0,), (0,)), ((), ())), preferred_element_type=F32)


def _bf(x):
    return x.astype(BF16)


def _split3(x):
    hi = x.astype(BF16)
    r1 = x - hi.astype(F32)
    mid = r1.astype(BF16)
    lo = (r1 - mid.astype(F32)).astype(BF16)
    return jnp.concatenate([hi, mid, lo], axis=0)


def _log_sigmoid(x):
    return jnp.minimum(x, 0.0) - _log1p_exp_neg_abs(x)


def _level_sizes(chunk):
    sizes, m = [], chunk // 2
    while m >= 1:
        sizes.append(m)
        m //= 2
    return sizes


def _hgrn2_tables(chunk):
    import numpy as np
    L = chunk
    idx = np.arange(L)
    mats = [(idx[None, :] <= idx[:, None]).astype(np.float32)]
    masks = []
    for m in _level_sizes(L):
        in_b = (idx % (2 * m)) >= m
        blk_start = (idx // m) * m
        blk_end = blk_start + m - 1
        c = idx[None, :]
        mat_b = (c >= blk_start[:, None]) & (c <= idx[:, None])
        mat_a = (c > idx[:, None]) & (c <= blk_end[:, None])
        mats.append(np.where(in_b[:, None], mat_b, mat_a).astype(np.float32))
        same_pair = (idx[:, None] // (2 * m)) == (idx[None, :] // (2 * m))
        masks.append((same_pair & in_b[:, None] & (~in_b)[None, :]).astype(np.float32))
    masks.append(np.eye(L, dtype=np.float32))
    pair_masks = np.stack([np.kron(np.eye(2, dtype=np.float32), m) for m in masks])
    mstack = np.concatenate(mats, axis=0)
    return np.concatenate([mstack] * 3, axis=1), pair_masks


HGRN2_UNROLL = 2
LOG_FLOOR = -1e30


def _hgrn2_kernel(x_ref, g_pre_ref, g_post_ref, win_ref, lb_ref, ng_ref, wout_ref, mstack_ref, mask_ref,
                  o_ref, u_scr, proj_scr, y_scr, q_scr, k_scr, xall_scr, st_scr, *, heads, dk, chunk, unroll):
    tb = x_ref.shape[1]
    hk = heads * dk
    n_lvl = len(_level_sizes(chunk))

    @pl.when(pl.program_id(1) == 0)
    def _():
        st_scr[...] = jnp.zeros_like(st_scr)

    u_scr[...] = _rms(x_ref[0], g_pre_ref[...]).astype(BF16)
    for j in range(4):
        proj_scr[:, j * hk:(j + 1) * hk] = _dot(u_scr[...], win_ref[:, j * hk:(j + 1) * hk])

    lb = lb_ref[...]

    def decay_phase(c, buf):
        rows = pl.ds(pl.multiple_of(c * chunk, chunk), chunk)
        ff = proj_scr[rows, hk:2 * hk]
        q_scr[buf] = _bf(_silu(proj_scr[rows, 0:hk]) * (dk ** -0.5))
        k = (1.0 - lb) * (0.5 - 0.5 * jnp.tanh(0.5 * ff))
        k_scr[buf] = _bf(k)
        log_f = jnp.maximum(jnp.log(1.0 - k), LOG_FLOOR)
        xall_scr[buf] = _dot(mstack_ref[...], _split3(log_f))

    def heads_phase(c, buf):
        rows = pl.ds(pl.multiple_of(c * chunk, chunk), chunk)
        pairs = [(h, h + 1) for h in range(0, heads, 2)]
        col = lambda h: slice(h * dk, (h + 1) * dk)
        stack = lambda f, p: jnp.concatenate([f(p[0]), f(p[1])], axis=0)
        q2 = [stack(lambda h: q_scr[buf, :, col(h)], p) for p in pairs]
        k2 = [stack(lambda h: k_scr[buf, :, col(h)], p) for p in pairs]
        attn = [None] * len(pairs)
        for lvl in range(n_lvl + 1):
            for i, p in enumerate(pairs):
                q, k = q2[i], k2[i]
                if lvl < n_lvl:
                    e = stack(lambda h: _bf(jnp.exp(
                        xall_scr[buf, (lvl + 1) * chunk:(lvl + 2) * chunk, col(h)])), p)
                    q, k = q * e, k * e
                prod = _dot_nt(q, k) * mask_ref[lvl]
                attn[i] = prod if lvl == 0 else attn[i] + prod
        hs = range(heads)
        g = [xall_scr[buf, 0:chunk, col(h)] for h in hs]
        v = [proj_scr[rows, 2 * hk + h * dk:2 * hk + (h + 1) * dk].astype(BF16) for h in hs]
        st = [st_scr[h] for h in hs]
        o_inter = [_dot_nt(q_scr[buf, :, col(h)] * _bf(jnp.exp(g[h])), st[h].astype(BF16)) for h in hs]
        kv = []
        for h in hs:
            g_last = g[h][chunk - 1:chunk, :]
            k_dec = k_scr[buf, :, col(h)] * _bf(jnp.exp(g_last - g[h]))
            kv.append(_dot_tn(v[h], k_dec))
        o_intra = [_dot(attn[i].astype(BF16), jnp.concatenate([v[p[0]], v[p[1]]], axis=0))
                   for i, p in enumerate(pairs)]
        for h in hs:
            st_scr[h] = st[h] * jnp.exp(g[h][chunk - 1:chunk, :]) + kv[h]
            o = o_inter[h] + o_intra[h // 2][(h % 2) * chunk:(h % 2 + 1) * chunk]
            gate = proj_scr[rows, 3 * hk + h * dk:3 * hk + (h + 1) * dk]
            y = (o * lax.rsqrt(jnp.mean(o * o, axis=-1, keepdims=True) + EPS) * ng_ref[:, col(h)]
                 * _silu(gate))
            y_scr[rows, col(h)] = y.astype(BF16)

    def chunks_body(i, carry):
        for j in range(unroll):
            decay_phase(i * unroll + j, j)
        for j in range(unroll):
            heads_phase(i * unroll + j, j)
        return carry

    lax.fori_loop(0, tb // chunk // unroll, chunks_body, 0)
    out = _dot(y_scr[...], wout_ref[...])
    o_ref[0] = x_ref[0] + _rms(out, g_post_ref[...])


def _hgrn2_mixer(x, g_pre, g_post, w_in, lower_bound, norm_gain, w_out, *, heads, block=1024, chunk=64):
    B, S, D = x.shape
    hk = w_out.shape[0]
    dk = hk // heads
    tb = min(block, S)
    unroll = min(HGRN2_UNROLL, tb // chunk)
    assert S % tb == 0 and tb % (chunk * unroll) == 0 and w_in.shape[1] == 4 * hk
    mstack, masks = _hgrn2_tables(chunk)
    n_rows = mstack.shape[0]
    kern = functools.partial(_hgrn2_kernel, heads=heads, dk=dk, chunk=chunk, unroll=unroll)
    return pl.pallas_call(
        kern,
        grid=(B, S // tb),
        in_specs=[
            pl.BlockSpec((1, tb, D), lambda b, s: (b, s, 0)),
            _resident((1, D)), _resident((1, D)),
            _resident((D, 4 * hk)), _resident((1, hk)), _resident((1, hk)), _resident((hk, D)),
            _resident(mstack.shape), _resident(masks.shape),
        ],
        out_specs=pl.BlockSpec((1, tb, D), lambda b, s: (b, s, 0)),
        out_shape=jax.ShapeDtypeStruct((B, S, D), F32),
        scratch_shapes=[
            pltpu.VMEM((tb, D), BF16),
            pltpu.VMEM((tb, 4 * hk), F32),
            pltpu.VMEM((tb, hk), BF16),
            pltpu.VMEM((unroll, chunk, hk), BF16),
            pltpu.VMEM((unroll, chunk, hk), BF16),
            pltpu.VMEM((unroll, n_rows, hk), F32),
            pltpu.VMEM((heads, dk, dk), F32),
        ],
        compiler_params=pltpu.CompilerParams(
            dimension_semantics=("arbitrary", "arbitrary"), vmem_limit_bytes=VMEM_LIMIT_BYTES),
        name="hgrn2_mixer",
    )(x, g_pre.reshape(1, D), g_post.reshape(1, D), w_in.astype(BF16), lower_bound.reshape(1, hk),
      norm_gain.reshape(1, hk), w_out.astype(BF16), jnp.asarray(mstack, BF16), jnp.asarray(masks, F32))


M_HEADS, M_DK, M_DV = 4, 64, 128
G_HEADS, G_DK, G_DV = 4, 128, 128
M_QK = M_HEADS * M_DK
M_V = M_HEADS * M_DV
G_QK = G_HEADS * G_DK
G_V = G_HEADS * G_DV
AB_CHUNK = 64
AB_GROUP = 8
AB_SEQ_UNROLL = 4
CONV_K = 4
GATE_SOFTCAP = 15.0
GATE_LANES = 128
CT_MQ = 0
CT_MK = CT_MQ + M_QK // LANES
CT_GQ = CT_MK + M_QK // LANES
CT_GK = CT_GQ + G_QK // LANES
CT_GV = CT_GK + G_QK // LANES
N_CONV_TILES = CT_GV + G_V // LANES
CONV_COLS = N_CONV_TILES * LANES
P_MV = 0
P_MO = P_MV + M_V
P_GG = P_MO + M_V
P_GATES = P_GG + G_V
PLAIN_COLS = P_GATES + GATE_LANES
AB_COLS = CONV_COLS + PLAIN_COLS
LANE_I, LANE_F, LANE_A, LANE_B = 0, M_HEADS, 2 * M_HEADS, 2 * M_HEADS + G_HEADS
ACT_MQ, ACT_MK = 0, M_QK
ACT_GQ = 2 * M_QK
ACT_GK = ACT_GQ + G_QK
ACT_GV = ACT_GK + G_QK
ACT_COLS = ACT_GV + G_V
N_MVEC = 3
N_MSC = 2


def _ab_tables(chunk):
    import numpy as np
    idx = np.arange(chunk)
    l, s = idx[:, None], idx[None, :]
    incl = l >= s
    strict = l > s
    blk16 = (l // 16) == (s // 16)
    blk32 = (l // 32) == (s // 32)
    masks = np.stack([incl, strict, blk16, blk32 & ~blk16, ~blk32, l == s]).astype(np.float32)
    tri3 = np.concatenate([incl.astype(np.float32)] * 3, axis=1)
    return tri3, masks


def _bcast_col(a, lane_ids, j):
    col = jnp.sum(jnp.where(lane_ids == j, a, 0.0), axis=-1, keepdims=True)
    return jnp.broadcast_to(col, a.shape)


def _unit_lower_inverses(ns, blk16, off32, off64, eye):
    d = [n * blk16 for n in ns]
    db = [_bf(x) for x in d]
    d2 = [_dot(x, x) for x in db]
    d2b = [_bf(x) for x in d2]
    p = [_dot(_bf(eye - a), _bf(eye + b)) for a, b in zip(d, d2)]
    d4 = [_dot(x, x) for x in d2b]
    d4b = [_bf(x) for x in d4]
    p = [_dot(_bf(a), _bf(eye + b)) for a, b in zip(p, d4)]
    d8 = [_dot(x, x) for x in d4b]
    p = [_dot(_bf(a), _bf(eye + b)) for a, b in zip(p, d8)]
    for off in (off32, off64):
        pb = [_bf(x) for x in p]
        t = [_dot(a, _bf(n * off)) for a, n in zip(pb, ns)]
        t = [_dot(_bf(a), b) for a, b in zip(t, pb)]
        p = [a - b for a, b in zip(p, t)]
    return p


def _ab_kernel(x_ref, g_pre_ref, g_post_ref, win_ref, cm_ref, cg_ref, gbias_ref, alog_ref, mn_ref, gn_ref,
               wout_ref, tri3_ref, mask_ref, o_ref,
               u_scr, conv_scr, proj_scr, act_scr, y_scr,
               mq_scr, mk_scr, mqk_scr, mlogd_scr, mvec_scr, msc_scr,
               gu_scr, gw_scr, gqd_scr, gkd_scr, gattn_scr, ggl_scr,
               cn_scr, m_scr, s_scr, *, chunk, group):
    tb = x_ref.shape[1]
    n_chunks = tb // chunk

    @pl.when(pl.program_id(1) == 0)
    def _():
        conv_scr[:, 0:HALO, :] = jnp.zeros((N_CONV_TILES, HALO, LANES), F32)
        cn_scr[...] = jnp.zeros_like(cn_scr)
        m_scr[...] = jnp.zeros_like(m_scr)
        s_scr[...] = jnp.zeros_like(s_scr)

    u_scr[...] = _bf(_rms(x_ref[0], g_pre_ref[...]))
    for c0 in range(0, CONV_COLS, 512):
        r = _dot(u_scr[...], win_ref[:, c0:c0 + 512])
        for lt in range(512 // LANES):
            conv_scr[c0 // LANES + lt, HALO:HALO + tb, :] = r[:, lt * LANES:(lt + 1) * LANES]
    for c0 in range(0, PLAIN_COLS, 512):
        c1 = min(c0 + 512, PLAIN_COLS)
        proj_scr[:, c0:c1] = _dot(u_scr[...], win_ref[:, CONV_COLS + c0:CONV_COLS + c1])

    def conv_silu(ci, lt, w_ref, wcol0):
        r0 = HALO + ci * chunk - (CONV_K - 1)
        acc = None
        for j in range(CONV_K):
            t = w_ref[j:j + 1, wcol0:wcol0 + LANES] * conv_scr[lt, r0 + j:r0 + j + chunk, :]
            acc = t if acc is None else acc + t
        return _silu(acc)

    def unit(v):
        return v * lax.rsqrt(jnp.sum(v * v, axis=-1, keepdims=True) + EPS)

    for ci in range(n_chunks):
        arow = slice(ci * chunk, (ci + 1) * chunk)
        for t in range(M_QK // LANES):
            mq = conv_silu(ci, CT_MQ + t, cm_ref, t * LANES)
            act_scr[arow, ACT_MQ + t * LANES:ACT_MQ + (t + 1) * LANES] = mq * (M_DK ** -0.5)
            mk = conv_silu(ci, CT_MK + t, cm_ref, M_QK + t * LANES)
            act_scr[arow, ACT_MK + t * LANES:ACT_MK + (t + 1) * LANES] = mk
        for h in range(G_HEADS):
            gq = conv_silu(ci, CT_GQ + h, cg_ref, h * G_DK)
            act_scr[arow, ACT_GQ + h * G_DK:ACT_GQ + (h + 1) * G_DK] = unit(gq) * (G_DK ** -0.5)
            gk = conv_silu(ci, CT_GK + h, cg_ref, G_QK + h * G_DK)
            act_scr[arow, ACT_GK + h * G_DK:ACT_GK + (h + 1) * G_DK] = unit(gk)
            gv = conv_silu(ci, CT_GV + h, cg_ref, 2 * G_QK + h * G_DV)
            act_scr[arow, ACT_GV + h * G_DV:ACT_GV + (h + 1) * G_DV] = gv
    conv_scr[:, 0:HALO, :] = conv_scr[:, tb:tb + HALO, :]

    lane = lax.broadcasted_iota(jnp.int32, (chunk, GATE_LANES), 1)
    gbias = gbias_ref[...]
    neg_a = -jnp.exp(alog_ref[...])

    def prep_body(g, carry):
        m_incl, m_strict, m_blk16, m_off32, m_off64, m_eye = (mask_ref[i] for i in range(6))
        ns, betas, gbs, ks, vs, arows = [], [], [], [], [], []
        for i in range(group):
            c = g * group + i
            a0 = pl.multiple_of(c * chunk, chunk)
            arow = pl.ds(a0, chunk)

            z = proj_scr[arow, P_GATES:P_GATES + GATE_LANES] + gbias
            capped = GATE_SOFTCAP * jnp.tanh(z / GATE_SOFTCAP)
            softplus = jnp.maximum(z, 0.0) + _log1p_exp_neg_abs(z)
            gp = jnp.where(lane < LANE_F, capped,
                           jnp.where(lane < LANE_A, _log_sigmoid(capped),
                                     jnp.where(lane < LANE_B, neg_a * softplus, _sigmoid(z))))
            cs = _dot(tri3_ref[...], _split3(gp))
            gp_t = gp.T
            cs_t = cs.T

            for h in range(M_HEADS):
                q = _bf(act_scr[arow, ACT_MQ + h * M_DK:ACT_MQ + (h + 1) * M_DK])
                k = act_scr[arow, ACT_MK + h * M_DK:ACT_MK + (h + 1) * M_DK]
                b_b = _bcast_col(cs, lane, LANE_F + h)
                ig_b = _bcast_col(gp, lane, LANE_I + h)
                b_row = cs_t[LANE_F + h:LANE_F + h + 1, :]
                ig_row = gp_t[LANE_I + h:LANE_I + h + 1, :]
                log_d = jnp.where(m_incl > 0.0, b_b[:, 0:chunk] - b_row + ig_row, -jnp.inf)
                b_last = b_b[chunk - 1:chunk, :]
                log_w = b_last - b_b + ig_b
                mq_scr[h, arow, :] = q
                mk_scr[h, arow, :] = k
                mqk_scr[h, arow, :] = _dot_nt(q, _bf(k))
                mlogd_scr[h, arow, :] = log_d
                mvec_scr[h * N_MVEC + 0, arow, :] = b_b
                mvec_scr[h * N_MVEC + 1, arow, :] = jnp.broadcast_to(
                    jnp.max(log_d, axis=-1, keepdims=True), (chunk, GATE_LANES))
                mvec_scr[h * N_MVEC + 2, arow, :] = log_w
                msc_scr[c, h * N_MSC + 0] = jnp.broadcast_to(b_last, (HALO, GATE_LANES))
                msc_scr[c, h * N_MSC + 1] = jnp.broadcast_to(
                    jnp.max(log_w, axis=0, keepdims=True), (HALO, GATE_LANES))

            for h in range(G_HEADS):
                q = act_scr[arow, ACT_GQ + h * G_DK:ACT_GQ + (h + 1) * G_DK]
                k = act_scr[arow, ACT_GK + h * G_DK:ACT_GK + (h + 1) * G_DK]
                v = act_scr[arow, ACT_GV + h * G_DV:ACT_GV + (h + 1) * G_DV]
                g_b = _bcast_col(cs, lane, LANE_A + h)
                beta = _bcast_col(gp, lane, LANE_B + h)
                g_row = cs_t[LANE_A + h:LANE_A + h + 1, :]
                decay = jnp.where(m_incl > 0.0,
                                  jnp.exp(jnp.where(m_incl > 0.0, g_b[:, 0:chunk] - g_row, 0.0)), 0.0)
                kb = _bf(k)
                ns.append(m_strict * (beta[:, 0:chunk] * _dot_nt(kb, kb) * decay))
                gattn_scr[h, arow, :] = _bf(_dot_nt(_bf(q), kb) * decay)
                g_last = g_b[chunk - 1:chunk, :]
                gqd_scr[h, arow, :] = _bf(q * jnp.exp(g_b))
                gkd_scr[h, arow, :] = _bf(k * jnp.exp(g_last - g_b))
                ggl_scr[c, h] = jnp.broadcast_to(jnp.exp(g_last), (HALO, GATE_LANES))
                betas.append(beta)
                gbs.append(g_b)
                ks.append(k)
                vs.append(v)
                arows.append((h, arow))

        t_inv = _unit_lower_inverses(ns, m_blk16, m_off32, m_off64, m_eye)
        for t, beta, g_b, k, v, (h, arow) in zip(t_inv, betas, gbs, ks, vs, arows):
            rhs = jnp.concatenate([v * beta, k * (beta * jnp.exp(g_b))], axis=1)
            sol = _dot(_bf(t), _bf(rhs))
            gu_scr[h, arow, :] = sol[:, 0:G_DV]
            gw_scr[h, arow, :] = _bf(sol[:, G_DV:G_DV + G_DK])
        return carry

    lax.fori_loop(0, n_chunks // group, prep_body, 0)

    ones_lane0 = _bf(jnp.where(lax.broadcasted_iota(jnp.int32, (chunk, M_DV), 1) == 0, 1.0, 0.0))

    def seq_body(c, carry):
        a0 = pl.multiple_of(c * chunk, chunk)
        arow = pl.ds(a0, chunk)

        m_t, w_inter, s, v_ext, kw, c_decay, m_new = [], [], [], [], [], [], []
        for h in range(M_HEADS):
            m_prev = m_scr[h, 0:1, :]
            b_b = mvec_scr[h * N_MVEC + 0, arow, :]
            b_last = msc_scr[c, h * N_MSC + 0, 0:1, :]
            log_inter = b_b + m_prev
            mt = jnp.maximum(mvec_scr[h * N_MVEC + 1, arow, :], log_inter)
            mn = jnp.maximum(b_last + m_prev, msc_scr[c, h * N_MSC + 1, 0:1, :])
            wk = jnp.exp(mvec_scr[h * N_MVEC + 2, arow, :] - mn)
            m_t.append(mt)
            w_inter.append(jnp.exp(log_inter - mt))
            s.append(_bf(mqk_scr[h, arow, :] * jnp.exp(mlogd_scr[h, arow, :] - mt[:, 0:chunk])))
            v = _bf(proj_scr[arow, P_MV + h * M_DV:P_MV + (h + 1) * M_DV])
            v_ext.append(jnp.concatenate([v, ones_lane0], axis=1))
            kw.append(_bf(mk_scr[h, arow, :] * wk[:, 0:M_DK]))
            c_decay.append(jnp.exp(b_last + m_prev - mn))
            m_new.append(mn)

        cn = [cn_scr[h] for h in range(M_HEADS)]
        st = [s_scr[h] for h in range(G_HEADS)]
        wq = [_dot(jnp.concatenate([gw_scr[h, arow, :], gqd_scr[h, arow, :]], axis=0), _bf(st[h]))
              for h in range(G_HEADS)]
        sv = [_dot(s[h], v_ext[h]) for h in range(M_HEADS)]
        qc = [_dot(mq_scr[h, arow, :], _bf(cn[h])) for h in range(M_HEADS)]
        upd = [_dot_tn(kw[h], v_ext[h]) for h in range(M_HEADS)]
        v_new = [_bf(gu_scr[h, arow, :] - wq[h][0:chunk]) for h in range(G_HEADS)]
        o_g = [wq[h][chunk:2 * chunk] + _dot(gattn_scr[h, arow, :], v_new[h]) for h in range(G_HEADS)]
        s_upd = [_dot_tn(gkd_scr[h, arow, :], v_new[h]) for h in range(G_HEADS)]

        for h in range(M_HEADS):
            wi = w_inter[h]
            nd = sv[h] + jnp.concatenate([wi, wi], axis=1) * qc[h]
            den = jnp.broadcast_to(nd[:, M_DV:M_DV + 1], (chunk, M_DV))
            hm = nd[:, 0:M_DV] / jnp.maximum(jnp.abs(den), jnp.exp(-m_t[h]))
            cd = c_decay[h]
            cn_scr[h] = jnp.concatenate([cd, cd], axis=1) * cn[h] + upd[h]
            m_scr[h] = jnp.broadcast_to(m_new[h], (HALO, GATE_LANES))
            gate = proj_scr[arow, P_MO + h * M_DV:P_MO + (h + 1) * M_DV]
            hn = hm * lax.rsqrt(jnp.mean(hm * hm, axis=-1, keepdims=True) + EPS)
            y_scr[arow, h * M_DV:(h + 1) * M_DV] = _bf(
                _sigmoid(gate) * hn * mn_ref[:, h * M_DV:(h + 1) * M_DV])

        for h in range(G_HEADS):
            s_scr[h] = ggl_scr[c, h, 0:1, :] * st[h] + s_upd[h]
            o = o_g[h]
            gate = proj_scr[arow, P_GG + h * G_DV:P_GG + (h + 1) * G_DV]
            on = o * lax.rsqrt(jnp.mean(o * o, axis=-1, keepdims=True) + EPS)
            y_scr[arow, M_V + h * G_DV:M_V + (h + 1) * G_DV] = _bf(
                on * gn_ref[:, h * G_DV:(h + 1) * G_DV] * _silu(gate))
        return carry

    def seq_trip(i, carry):
        for j in range(AB_SEQ_UNROLL):
            seq_body(i * AB_SEQ_UNROLL + j, carry)
        return carry

    lax.fori_loop(0, n_chunks // AB_SEQ_UNROLL, seq_trip, 0)
    out = _dot(y_scr[...], wout_ref[...])
    o_ref[0] = x_ref[0] + _rms(out, g_post_ref[...])


def _ab_pack_w_in(w_in):
    sizes = (M_QK, M_QK, M_V, M_V, M_HEADS, M_HEADS, G_QK, G_QK, G_V, G_V, G_HEADS, G_HEADS)
    offs = [0]
    for sz in sizes:
        offs.append(offs[-1] + sz)
    seg = [w_in[:, offs[i]:offs[i + 1]] for i in range(len(sizes))]
    m_q, m_k, m_v, m_o, m_i, m_f, g_q, g_k, g_v, g_gate, g_a, g_b = seg
    gates = jnp.concatenate([m_i, m_f, g_a, g_b], axis=1)
    gates = jnp.pad(gates, ((0, 0), (0, GATE_LANES - gates.shape[1])))
    return jnp.concatenate([m_q, m_k, g_q, g_k, g_v, m_v, m_o, g_gate, gates], axis=1).astype(BF16)


def _lane_vec(pieces):
    v = jnp.zeros((GATE_LANES,), F32)
    for off, vals in pieces:
        v = lax.dynamic_update_slice(v, vals.astype(F32), (off,))
    return v.reshape(1, GATE_LANES)


def _ab_mixer(x, g_pre, g_post, w_in, conv_m, m_gate_bias, m_norm, conv_g, g_a_log, g_dt_bias, g_norm, w_out,
              *, block=512):
    B, S, D = x.shape
    chunk = AB_CHUNK
    tb = min(block, S)
    n_chunks = tb // chunk
    group = min(AB_GROUP, n_chunks)
    assert S % tb == 0 and tb % chunk == 0 and n_chunks % group == 0
    tri3, masks = _ab_tables(chunk)
    gbias = _lane_vec([(LANE_I, m_gate_bias[0]), (LANE_F, m_gate_bias[1]), (LANE_A, g_dt_bias)])
    alog = _lane_vec([(LANE_A, g_a_log)])
    kern = functools.partial(_ab_kernel, chunk=chunk, group=group)
    mix = M_V + G_V
    return pl.pallas_call(
        kern,
        grid=(B, S // tb),
        in_specs=[
            pl.BlockSpec((1, tb, D), lambda b, s: (b, s, 0)),
            _resident((1, D)), _resident((1, D)),
            _resident((D, AB_COLS)), _resident(conv_m.shape), _resident(conv_g.shape),
            _resident((1, GATE_LANES)), _resident((1, GATE_LANES)),
            _resident((1, M_V)), _resident((1, G_V)), _resident((mix, D)),
            _resident(tri3.shape), _resident(masks.shape),
        ],
        out_specs=pl.BlockSpec((1, tb, D), lambda b, s: (b, s, 0)),
        out_shape=jax.ShapeDtypeStruct((B, S, D), F32),
        scratch_shapes=[
            pltpu.VMEM((tb, D), BF16),
            pltpu.VMEM((N_CONV_TILES, HALO + tb, LANES), F32),
            pltpu.VMEM((tb, PLAIN_COLS), F32),
            pltpu.VMEM((tb, ACT_COLS), F32),
            pltpu.VMEM((tb, mix), BF16),
            pltpu.VMEM((M_HEADS, tb, M_DK), BF16),
            pltpu.VMEM((M_HEADS, tb, M_DK), F32),
            pltpu.VMEM((M_HEADS, tb, chunk), F32),
            pltpu.VMEM((M_HEADS, tb, chunk), F32),
            pltpu.VMEM((M_HEADS * N_MVEC, tb, GATE_LANES), F32),
            pltpu.VMEM((n_chunks, M_HEADS * N_MSC, HALO, GATE_LANES), F32),
            pltpu.VMEM((G_HEADS, tb, G_DV), F32),
            pltpu.VMEM((G_HEADS, tb, G_DK), BF16),
            pltpu.VMEM((G_HEADS, tb, G_DK), BF16),
            pltpu.VMEM((G_HEADS, tb, G_DK), BF16),
            pltpu.VMEM((G_HEADS, tb, chunk), BF16),
            pltpu.VMEM((n_chunks, G_HEADS, HALO, GATE_LANES), F32),
            pltpu.VMEM((M_HEADS, M_DK, M_DV + 128), F32),
            pltpu.VMEM((M_HEADS, HALO, GATE_LANES), F32),
            pltpu.VMEM((G_HEADS, G_DK, G_DV), F32),
        ],
        compiler_params=pltpu.CompilerParams(
            dimension_semantics=("arbitrary", "arbitrary"), vmem_limit_bytes=VMEM_LIMIT_BYTES),
        name="ab_mixer",
    )(x, g_pre.reshape(1, D), g_post.reshape(1, D), _ab_pack_w_in(w_in), conv_m, conv_g, gbias, alog,
      m_norm.reshape(1, M_V), g_norm.reshape(1, G_V), w_out.astype(BF16),
      jnp.asarray(tri3, BF16), jnp.asarray(masks, F32))


def kernel(x, norm_gains, ab_w_in, ab_conv_m, ab_m_gate_bias, ab_m_norm, ab_conv_g, ab_g_a_log, ab_g_dt_bias, ab_g_norm, ab_w_out, c_w_in, c_lb_logits, c_norm, c_w_out, ffn_w_up, ffn_conv_w, ffn_conv_b, ffn_w_down):
    depth = norm_gains.shape[0]
    p = jax.nn.softmax(c_lb_logits.astype(F32), axis=0)
    lower_bounds = jnp.cumsum(p, axis=0) - p[0]
    c_heads = c_w_out.shape[1] // 128
    for layer in range(depth):
        j = layer // 2
        g = norm_gains[layer]
        if layer % 2 == 0:
            x = _ab_mixer(x, g[0], g[1], ab_w_in[j], ab_conv_m[j], ab_m_gate_bias[j], ab_m_norm[j],
                          ab_conv_g[j], ab_g_a_log[j], ab_g_dt_bias[j], ab_g_norm[j], ab_w_out[j])
        else:
            x = _hgrn2_mixer(x, g[0], g[1], c_w_in[j], lower_bounds[layer], c_norm[j], c_w_out[j],
                             heads=c_heads)
        x = _ffn(x, g[2], g[3], ffn_w_up[layer], ffn_conv_w[layer], ffn_conv_b[layer], ffn_w_down[layer])
    return x
```

```python
import functools

import jax
import jax.numpy as jnp
from jax import lax
from jax.experimental import pallas as pl
from jax.experimental.pallas import tpu as pltpu

F32 = jnp.float32
BF16 = jnp.bfloat16
EPS = 1e-6
LANES = 128
HALO = 8
FFN_DOWN_LAG = 2
FFN_DOWN_PART = 4
FFN_CONV_K = 3
ROW_TILE = 128
VMEM_LIMIT_BYTES = 56 * 1024 * 1024


def _rms(x, gain):
    return x * lax.rsqrt(jnp.mean(x * x, axis=-1, keepdims=True) + EPS) * gain


def _prenorm_to(u_scr, x_ref, gain_ref):
    tb = x_ref.shape[1]
    rt = min(ROW_TILE, tb)
    for r0 in range(0, tb, rt):
        u_scr[r0:r0 + rt, :] = _rms(x_ref[0, r0:r0 + rt, :], gain_ref[...]).astype(BF16)


def _residual_postnorm(o_ref, x_ref, gain_ref):
    tb = x_ref.shape[1]
    rt = min(ROW_TILE, tb)
    for r0 in range(0, tb, rt):
        rows = slice(r0, r0 + rt)
        o_ref[0, rows, :] = x_ref[0, rows, :] + _rms(o_ref[0, rows, :], gain_ref[...])


def _sigmoid(x):
    return 0.5 * jnp.tanh(0.5 * x) + 0.5


def _silu(x):
    return x * _sigmoid(x)


def _log1p_exp_neg_abs(x):
    return jnp.log(1.0 + jnp.exp(-jnp.abs(x)))


def _resident(shape):
    nd = len(shape)
    return pl.BlockSpec(shape, lambda b, s: (0,) * nd, pipeline_mode=pl.Buffered(1))


def _ffn_kernel(x_ref, g_pre_ref, g_post_ref, wup_ref, cw_ref, cb_ref, wdn_ref, o_ref,
                u_scr, a_scr, h_scr, carry_scr, *, d_ff, slab):
    tb = x_ref.shape[1]

    @pl.when(pl.program_id(1) == 0)
    def _():
        carry_scr[...] = jnp.zeros_like(carry_scr)

    n_slab = d_ff // slab
    rt = min(ROW_TILE, tb)
    _prenorm_to(u_scr, x_ref, g_pre_ref)

    def up_half(buf, c0):
        h = jnp.dot(u_scr[...], wup_ref[:, c0:c0 + slab], preferred_element_type=F32)
        for lt in range(slab // LANES):
            cols = slice(c0 + lt * LANES, c0 + (lt + 1) * LANES)
            h_scr[buf, lt, 0:HALO, :] = carry_scr[:, cols]
            h_scr[buf, lt, HALO:HALO + tb, :] = h[:, lt * LANES:(lt + 1) * LANES]
            carry_scr[:, cols] = h_scr[buf, lt, tb:tb + HALO, :]

    def conv_tile(buf, lt, c0, r0):
        cols = slice(c0 + lt * LANES, c0 + (lt + 1) * LANES)
        w = cw_ref[:, cols]
        acc = cb_ref[:, cols]
        for j in range(FFN_CONV_K):
            r = HALO + r0 - (FFN_CONV_K - 1) + j
            acc = acc + w[j:j + 1] * h_scr[buf, lt, r:r + rt, :]
        return acc

    done = 0

    def down_part(lo, hi):
        part = jnp.dot(a_scr[:, lo * slab:hi * slab], wdn_ref[lo * slab:hi * slab, :],
                       preferred_element_type=F32)
        o_ref[0] = part if lo == 0 else o_ref[0] + part

    for j in range(n_slab):
        bg, bv = 2 * (j % 2), 2 * (j % 2) + 1
        up_half(bg, j * slab)
        up_half(bv, d_ff + j * slab)
        for lt in range(slab // LANES):
            for r0 in range(0, tb, rt):
                g = conv_tile(bg, lt, j * slab, r0)
                v = conv_tile(bv, lt, d_ff + j * slab, r0)
                a_scr[r0:r0 + rt, j * slab + lt * LANES:j * slab + (lt + 1) * LANES] = (
                    _silu(g) * v).astype(BF16)
        ready = j + 1 - FFN_DOWN_LAG
        if ready - done >= FFN_DOWN_PART or (j == n_slab - 1 and ready > done):
            down_part(done, ready)
            done = ready
    if done < n_slab:
        down_part(done, n_slab)
    _residual_postnorm(o_ref, x_ref, g_post_ref)


def _ffn(x, g_pre, g_post, w_up, conv_w, conv_b, w_down, *, block=1024, slab=256):
    B, S, D = x.shape
    d_ff = w_down.shape[0]
    tb = min(block, S)
    assert S % tb == 0 and d_ff % slab == 0
    kern = functools.partial(_ffn_kernel, d_ff=d_ff, slab=slab)
    return pl.pallas_call(
        kern,
        grid=(B, S // tb),
        in_specs=[
            pl.BlockSpec((1, tb, D), lambda b, s: (b, s, 0)),
            _resident((1, D)), _resident((1, D)),
            _resident((D, 2 * d_ff)), _resident((3, 2 * d_ff)), _resident((1, 2 * d_ff)),
            _resident((d_ff, D)),
        ],
        out_specs=pl.BlockSpec((1, tb, D), lambda b, s: (b, s, 0)),
        out_shape=jax.ShapeDtypeStruct((B, S, D), F32),
        scratch_shapes=[
            pltpu.VMEM((tb, D), BF16),
            pltpu.VMEM((tb, d_ff), BF16),
            pltpu.VMEM((4, slab // LANES, HALO + tb, LANES), F32),
            pltpu.VMEM((HALO, 2 * d_ff), F32),
        ],
        compiler_params=pltpu.CompilerParams(
            dimension_semantics=("arbitrary", "arbitrary"), vmem_limit_bytes=VMEM_LIMIT_BYTES),
        name="ffn",
    )(x, g_pre.reshape(1, D), g_post.reshape(1, D), w_up.astype(BF16), conv_w,
      conv_b.reshape(1, 2 * d_ff), w_down.astype(BF16))


def _dot(a, b):
    return jnp.dot(a, b, preferred_element_type=F32)


def _dot_nt(a, b):
    return lax.dot_general(a, b, (((1,), (1,)), ((), ())), preferred_element_type=F32)


def _dot_tn(a, b):
    return lax.dot_general(a, b, (((0,), (0,)), ((), ())), preferred_element_type=F32)


def _bf(x):
    return x.astype(BF16)


def _split3(x):
    hi = x.astype(BF16)
    r1 = x - hi.astype(F32)
    mid = r1.astype(BF16)
    lo = (r1 - mid.astype(F32)).astype(BF16)
    return jnp.concatenate([hi, mid, lo], axis=0)


def _log_sigmoid(x):
    return jnp.minimum(x, 0.0) - _log1p_exp_neg_abs(x)


def _level_sizes(chunk):
    sizes, m = [], chunk // 2
    while m >= 1:
        sizes.append(m)
        m //= 2
    return sizes


def _hgrn2_tables(chunk):
    import numpy as np
    L = chunk
    idx = np.arange(L)
    mats = [(idx[None, :] <= idx[:, None]).astype(np.float32)]
    masks = []
    for m in _level_sizes(L):
        in_b = (idx % (2 * m)) >= m
        blk_start = (idx // m) * m
        blk_end = blk_start + m - 1
        c = idx[None, :]
        mat_b = (c >= blk_start[:, None]) & (c <= idx[:, None])
        mat_a = (c > idx[:, None]) & (c <= blk_end[:, None])
        mats.append(np.where(in_b[:, None], mat_b, mat_a).astype(np.float32))
        same_pair = (idx[:, None] // (2 * m)) == (idx[None, :] // (2 * m))
        masks.append((same_pair & in_b[:, None] & (~in_b)[None, :]).astype(np.float32))
    masks.append(np.eye(L, dtype=np.float32))
    pair_masks = np.stack([np.kron(np.eye(2, dtype=np.float32), m) for m in masks])
    mstack = np.concatenate(mats, axis=0)
    return np.concatenate([mstack] * 3, axis=1), pair_masks


HGRN2_UNROLL = 2
HGRN2_COL_SLAB = 256
LOG_FLOOR = -1e30


def _hgrn2_kernel(x_ref, g_pre_ref, g_post_ref, win_ref, lb_ref, ng_ref, wout_ref, mstack_ref, mask_ref,
                  o_ref, u_scr, proj_scr, y_scr, q_scr, k_scr, lf3_scr, xall_scr, st_scr, *, heads, dk, chunk, unroll):
    tb = x_ref.shape[1]
    hk = heads * dk
    n_lvl = len(_level_sizes(chunk))

    @pl.when(pl.program_id(1) == 0)
    def _():
        st_scr[...] = jnp.zeros_like(st_scr)

    _prenorm_to(u_scr, x_ref, g_pre_ref)
    for j in range(4):
        proj_scr[:, j * hk:(j + 1) * hk] = _dot(u_scr[...], win_ref[:, j * hk:(j + 1) * hk])

    def decay_phase(c, buf):
        rows = pl.ds(pl.multiple_of(c * chunk, chunk), chunk)
        for c0 in range(0, hk, HGRN2_COL_SLAB):
            cs = slice(c0, c0 + HGRN2_COL_SLAB)
            ff = proj_scr[rows, hk + c0:hk + c0 + HGRN2_COL_SLAB]
            q_scr[buf, :, cs] = _bf(_silu(proj_scr[rows, cs]) * (dk ** -0.5))
            k = (1.0 - lb_ref[:, cs]) * (0.5 - 0.5 * jnp.tanh(0.5 * ff))
            k_scr[buf, :, cs] = _bf(k)
            lf3_scr[buf, :, cs] = _split3(jnp.maximum(jnp.log(1.0 - k), LOG_FLOOR))
        xall_scr[buf] = _dot(mstack_ref[...], lf3_scr[buf])

    def heads_phase(c, buf):
        rows = pl.ds(pl.multiple_of(c * chunk, chunk), chunk)
        pairs = [(h, h + 1) for h in range(0, heads, 2)]
        col = lambda h: slice(h * dk, (h + 1) * dk)
        stack = lambda f, p: jnp.concatenate([f(p[0]), f(p[1])], axis=0)
        q2 = [stack(lambda h: q_scr[buf, :, col(h)], p) for p in pairs]
        k2 = [stack(lambda h: k_scr[buf, :, col(h)], p) for p in pairs]
        attn = [None] * len(pairs)
        for lvl in range(n_lvl + 1):
            for i, p in enumerate(pairs):
                q, k = q2[i], k2[i]
                if lvl < n_lvl:
                    e = stack(lambda h: _bf(jnp.exp(
                        xall_scr[buf, (lvl + 1) * chunk:(lvl + 2) * chunk, col(h)])), p)
                    q, k = q * e, k * e
                prod = _dot_nt(q, k) * mask_ref[lvl]
                attn[i] = prod if lvl == 0 else attn[i] + prod
        hs = range(heads)
        g = [xall_scr[buf, 0:chunk, col(h)] for h in hs]
        v = [proj_scr[rows, 2 * hk + h * dk:2 * hk + (h + 1) * dk].astype(BF16) for h in hs]
        st = [st_scr[h] for h in hs]
        o_inter = [_dot_nt(q_scr[buf, :, col(h)] * _bf(jnp.exp(g[h])), st[h].astype(BF16)) for h in hs]
        kv = []
        for h in hs:
            g_last = g[h][chunk - 1:chunk, :]
            k_dec = k_scr[buf, :, col(h)] * _bf(jnp.exp(g_last - g[h]))
            kv.append(_dot_tn(v[h], k_dec))
        o_intra = [_dot(attn[i].astype(BF16), jnp.concatenate([v[p[0]], v[p[1]]], axis=0))
                   for i, p in enumerate(pairs)]
        for h in hs:
            st_scr[h] = st[h] * jnp.exp(g[h][chunk - 1:chunk, :]) + kv[h]
            o = o_inter[h] + o_intra[h // 2][(h % 2) * chunk:(h % 2 + 1) * chunk]
            gate = proj_scr[rows, 3 * hk + h * dk:3 * hk + (h + 1) * dk]
            y = (o * lax.rsqrt(jnp.mean(o * o, axis=-1, keepdims=True) + EPS) * ng_ref[:, col(h)]
                 * _silu(gate))
            y_scr[rows, col(h)] = y.astype(BF16)

    def chunks_body(i, carry):
        for j in range(unroll):
            decay_phase(i * unroll + j, j)
        for j in range(unroll):
            heads_phase(i * unroll + j, j)
        return carry

    lax.fori_loop(0, tb // chunk // unroll, chunks_body, 0)
    o_ref[0] = _dot(y_scr[...], wout_ref[...])
    _residual_postnorm(o_ref, x_ref, g_post_ref)


def _hgrn2_mixer(x, g_pre, g_post, w_in, lower_bound, norm_gain, w_out, *, heads, block=1024, chunk=64):
    B, S, D = x.shape
    hk = w_out.shape[0]
    dk = hk // heads
    tb = min(block, S)
    unroll = min(HGRN2_UNROLL, tb // chunk)
    assert S % tb == 0 and tb % (chunk * unroll) == 0 and w_in.shape[1] == 4 * hk
    mstack, masks = _hgrn2_tables(chunk)
    n_rows = mstack.shape[0]
    kern = functools.partial(_hgrn2_kernel, heads=heads, dk=dk, chunk=chunk, unroll=unroll)
    return pl.pallas_call(
        kern,
        grid=(B, S // tb),
        in_specs=[
            pl.BlockSpec((1, tb, D), lambda b, s: (b, s, 0)),
            _resident((1, D)), _resident((1, D)),
            _resident((D, 4 * hk)), _resident((1, hk)), _resident((1, hk)), _resident((hk, D)),
            _resident(mstack.shape), _resident(masks.shape),
        ],
        out_specs=pl.BlockSpec((1, tb, D), lambda b, s: (b, s, 0)),
        out_shape=jax.ShapeDtypeStruct((B, S, D), F32),
        scratch_shapes=[
            pltpu.VMEM((tb, D), BF16),
            pltpu.VMEM((tb, 4 * hk), F32),
            pltpu.VMEM((tb, hk), BF16),
            pltpu.VMEM((unroll, chunk, hk), BF16),
            pltpu.VMEM((unroll, chunk, hk), BF16),
            pltpu.VMEM((unroll, 3 * chunk, hk), BF16),
            pltpu.VMEM((unroll, n_rows, hk), F32),
            pltpu.VMEM((heads, dk, dk), F32),
        ],
        compiler_params=pltpu.CompilerParams(
            dimension_semantics=("arbitrary", "arbitrary"), vmem_limit_bytes=VMEM_LIMIT_BYTES),
        name="hgrn2_mixer",
    )(x, g_pre.reshape(1, D), g_post.reshape(1, D), w_in.astype(BF16), lower_bound.reshape(1, hk),
      norm_gain.reshape(1, hk), w_out.astype(BF16), jnp.asarray(mstack, BF16), jnp.asarray(masks, F32))


M_HEADS, M_DK, M_DV = 4, 64, 128
G_HEADS, G_DK, G_DV = 4, 128, 128
M_QK = M_HEADS * M_DK
M_V = M_HEADS * M_DV
G_QK = G_HEADS * G_DK
G_V = G_HEADS * G_DV
AB_CHUNK = 64
AB_GROUP = 8
AB_SEQ_UNROLL = 4
CONV_K = 4
GATE_SOFTCAP = 15.0
GATE_LANES = 128
CT_MQ = 0
CT_MK = CT_MQ + M_QK // LANES
CT_GQ = CT_MK + M_QK // LANES
CT_GK = CT_GQ + G_QK // LANES
CT_GV = CT_GK + G_QK // LANES
N_CONV_TILES = CT_GV + G_V // LANES
CONV_COLS = N_CONV_TILES * LANES
P_MV = 0
P_MO = P_MV + M_V
P_GG = P_MO + M_V
P_GATES = P_GG + G_V
PLAIN_COLS = P_GATES + GATE_LANES
AB_COLS = CONV_COLS + PLAIN_COLS
LANE_I, LANE_F, LANE_A, LANE_B = 0, M_HEADS, 2 * M_HEADS, 2 * M_HEADS + G_HEADS
ACT_MQ, ACT_MK = 0, M_QK
ACT_GQ = 2 * M_QK
ACT_GK = ACT_GQ + G_QK
ACT_GV = ACT_GK + G_QK
ACT_COLS = ACT_GV + G_V
N_MVEC = 3
N_MSC = 2


def _ab_tables(chunk):
    import numpy as np
    idx = np.arange(chunk)
    l, s = idx[:, None], idx[None, :]
    incl = l >= s
    strict = l > s
    blk16 = (l // 16) == (s // 16)
    blk32 = (l // 32) == (s // 32)
    masks = np.stack([incl, strict, blk16, blk32 & ~blk16, ~blk32, l == s]).astype(np.float32)
    tri3 = np.concatenate([incl.astype(np.float32)] * 3, axis=1)
    return tri3, masks


def _bcast_col(a, lane_ids, j):
    col = jnp.sum(jnp.where(lane_ids == j, a, 0.0), axis=-1, keepdims=True)
    return jnp.broadcast_to(col, a.shape)


def _unit_lower_inverses(ns, blk16, off32, off64, eye):
    d = [n * blk16 for n in ns]
    db = [_bf(x) for x in d]
    d2 = [_dot(x, x) for x in db]
    d2b = [_bf(x) for x in d2]
    p = [_dot(_bf(eye - a), _bf(eye + b)) for a, b in zip(d, d2)]
    d4 = [_dot(x, x) for x in d2b]
    d4b = [_bf(x) for x in d4]
    p = [_dot(_bf(a), _bf(eye + b)) for a, b in zip(p, d4)]
    d8 = [_dot(x, x) for x in d4b]
    p = [_dot(_bf(a), _bf(eye + b)) for a, b in zip(p, d8)]
    for off in (off32, off64):
        pb = [_bf(x) for x in p]
        t = [_dot(a, _bf(n * off)) for a, n in zip(pb, ns)]
        t = [_dot(_bf(a), b) for a, b in zip(t, pb)]
        p = [a - b for a, b in zip(p, t)]
    return p


def _ab_kernel(x_ref, g_pre_ref, g_post_ref, win_ref, cm_ref, cg_ref, gbias_ref, alog_ref, mn_ref, gn_ref,
               wout_ref, tri3_ref, mask_ref, o_ref,
               u_scr, conv_scr, proj_scr, act_scr, y_scr,
               mq_scr, mk_scr, mqk_scr, mlogd_scr, mvec_scr, msc_scr,
               gu_scr, gw_scr, gqd_scr, gkd_scr, gattn_scr, ggl_scr,
               cn_scr, m_scr, s_scr, *, chunk, group, seq_unroll):
    tb = x_ref.shape[1]
    n_chunks = tb // chunk

    @pl.when(pl.program_id(1) == 0)
    def _():
        conv_scr[:, 0:HALO, :] = jnp.zeros((N_CONV_TILES, HALO, LANES), F32)
        cn_scr[...] = jnp.zeros_like(cn_scr)
        m_scr[...] = jnp.zeros_like(m_scr)
        s_scr[...] = jnp.zeros_like(s_scr)

    _prenorm_to(u_scr, x_ref, g_pre_ref)
    for c0 in range(0, CONV_COLS, 512):
        r = _dot(u_scr[...], win_ref[:, c0:c0 + 512])
        for lt in range(512 // LANES):
            conv_scr[c0 // LANES + lt, HALO:HALO + tb, :] = r[:, lt * LANES:(lt + 1) * LANES]
    for c0 in range(0, PLAIN_COLS, 512):
        c1 = min(c0 + 512, PLAIN_COLS)
        proj_scr[:, c0:c1] = _dot(u_scr[...], win_ref[:, CONV_COLS + c0:CONV_COLS + c1])

    def conv_silu(ci, lt, w_ref, wcol0):
        r0 = HALO + ci * chunk - (CONV_K - 1)
        acc = None
        for j in range(CONV_K):
            t = w_ref[j:j + 1, wcol0:wcol0 + LANES] * conv_scr[lt, r0 + j:r0 + j + chunk, :]
            acc = t if acc is None else acc + t
        return _silu(acc)

    def unit(v):
        return v * lax.rsqrt(jnp.sum(v * v, axis=-1, keepdims=True) + EPS)

    for ci in range(n_chunks):
        arow = slice(ci * chunk, (ci + 1) * chunk)
        for t in range(M_QK // LANES):
            mq = conv_silu(ci, CT_MQ + t, cm_ref, t * LANES)
            act_scr[arow, ACT_MQ + t * LANES:ACT_MQ + (t + 1) * LANES] = mq * (M_DK ** -0.5)
            mk = conv_silu(ci, CT_MK + t, cm_ref, M_QK + t * LANES)
            act_scr[arow, ACT_MK + t * LANES:ACT_MK + (t + 1) * LANES] = mk
        for h in range(G_HEADS):
            gq = conv_silu(ci, CT_GQ + h, cg_ref, h * G_DK)
            act_scr[arow, ACT_GQ + h * G_DK:ACT_GQ + (h + 1) * G_DK] = unit(gq) * (G_DK ** -0.5)
            gk = conv_silu(ci, CT_GK + h, cg_ref, G_QK + h * G_DK)
            act_scr[arow, ACT_GK + h * G_DK:ACT_GK + (h + 1) * G_DK] = unit(gk)
            gv = conv_silu(ci, CT_GV + h, cg_ref, 2 * G_QK + h * G_DV)
            act_scr[arow, ACT_GV + h * G_DV:ACT_GV + (h + 1) * G_DV] = gv
    conv_scr[:, 0:HALO, :] = conv_scr[:, tb:tb + HALO, :]

    lane = lax.broadcasted_iota(jnp.int32, (chunk, GATE_LANES), 1)
    gbias = gbias_ref[...]
    neg_a = -jnp.exp(alog_ref[...])

    def prep_body(g, carry):
        m_incl, m_strict, m_blk16, m_off32, m_off64, m_eye = (mask_ref[i] for i in range(6))
        ns, betas, gbs, ks, vs, arows = [], [], [], [], [], []
        for i in range(group):
            c = g * group + i
            a0 = pl.multiple_of(c * chunk, chunk)
            arow = pl.ds(a0, chunk)

            z = proj_scr[arow, P_GATES:P_GATES + GATE_LANES] + gbias
            capped = GATE_SOFTCAP * jnp.tanh(z / GATE_SOFTCAP)
            softplus = jnp.maximum(z, 0.0) + _log1p_exp_neg_abs(z)
            gp = jnp.where(lane < LANE_F, capped,
                           jnp.where(lane < LANE_A, _log_sigmoid(capped),
                                     jnp.where(lane < LANE_B, neg_a * softplus, _sigmoid(z))))
            cs = _dot(tri3_ref[...], _split3(gp))
            gp_t = gp.T
            cs_t = cs.T

            for h in range(M_HEADS):
                q = _bf(act_scr[arow, ACT_MQ + h * M_DK:ACT_MQ + (h + 1) * M_DK])
                k = act_scr[arow, ACT_MK + h * M_DK:ACT_MK + (h + 1) * M_DK]
                b_b = _bcast_col(cs, lane, LANE_F + h)
                ig_b = _bcast_col(gp, lane, LANE_I + h)
                b_row = cs_t[LANE_F + h:LANE_F + h + 1, :]
                ig_row = gp_t[LANE_I + h:LANE_I + h + 1, :]
                log_d = jnp.where(m_incl > 0.0, b_b[:, 0:chunk] - b_row + ig_row, -jnp.inf)
                b_last = b_b[chunk - 1:chunk, :]
                log_w = b_last - b_b + ig_b
                mq_scr[h, arow, :] = q
                mk_scr[h, arow, :] = k
                mqk_scr[h, arow, :] = _dot_nt(q, _bf(k))
                mlogd_scr[h, arow, :] = log_d
                mvec_scr[h * N_MVEC + 0, arow, :] = b_b
                mvec_scr[h * N_MVEC + 1, arow, :] = jnp.broadcast_to(
                    jnp.max(log_d, axis=-1, keepdims=True), (chunk, GATE_LANES))
                mvec_scr[h * N_MVEC + 2, arow, :] = log_w
                msc_scr[c, h * N_MSC + 0] = jnp.broadcast_to(b_last, (HALO, GATE_LANES))
                msc_scr[c, h * N_MSC + 1] = jnp.broadcast_to(
                    jnp.max(log_w, axis=0, keepdims=True), (HALO, GATE_LANES))

            for h in range(G_HEADS):
                q = act_scr[arow, ACT_GQ + h * G_DK:ACT_GQ + (h + 1) * G_DK]
                k = act_scr[arow, ACT_GK + h * G_DK:ACT_GK + (h + 1) * G_DK]
                v = act_scr[arow, ACT_GV + h * G_DV:ACT_GV + (h + 1) * G_DV]
                g_b = _bcast_col(cs, lane, LANE_A + h)
                beta = _bcast_col(gp, lane, LANE_B + h)
                g_row = cs_t[LANE_A + h:LANE_A + h + 1, :]
                decay = jnp.where(m_incl > 0.0,
                                  jnp.exp(jnp.where(m_incl > 0.0, g_b[:, 0:chunk] - g_row, 0.0)), 0.0)
                kb = _bf(k)
                ns.append(m_strict * (beta[:, 0:chunk] * _dot_nt(kb, kb) * decay))
                gattn_scr[h, arow, :] = _bf(_dot_nt(_bf(q), kb) * decay)
                g_last = g_b[chunk - 1:chunk, :]
                gqd_scr[h, arow, :] = _bf(q * jnp.exp(g_b))
                gkd_scr[h, arow, :] = _bf(k * jnp.exp(g_last - g_b))
                ggl_scr[c, h] = jnp.broadcast_to(jnp.exp(g_last), (HALO, GATE_LANES))
                betas.append(beta)
                gbs.append(g_b)
                ks.append(k)
                vs.append(v)
                arows.append((h, arow))

        t_inv = _unit_lower_inverses(ns, m_blk16, m_off32, m_off64, m_eye)
        for t, beta, g_b, k, v, (h, arow) in zip(t_inv, betas, gbs, ks, vs, arows):
            rhs = jnp.concatenate([v * beta, k * (beta * jnp.exp(g_b))], axis=1)
            sol = _dot(_bf(t), _bf(rhs))
            gu_scr[h, arow, :] = sol[:, 0:G_DV]
            gw_scr[h, arow, :] = _bf(sol[:, G_DV:G_DV + G_DK])
        return carry

    lax.fori_loop(0, n_chunks // group, prep_body, 0)

    ones_lane0 = _bf(jnp.where(lax.broadcasted_iota(jnp.int32, (chunk, M_DV), 1) == 0, 1.0, 0.0))

    def seq_body(c, carry):
        a0 = pl.multiple_of(c * chunk, chunk)
        arow = pl.ds(a0, chunk)

        m_t, w_inter, s, v_ext, kw, c_decay, m_new = [], [], [], [], [], [], []
        for h in range(M_HEADS):
            m_prev = m_scr[h, 0:1, :]
            b_b = mvec_scr[h * N_MVEC + 0, arow, :]
            b_last = msc_scr[c, h * N_MSC + 0, 0:1, :]
            log_inter = b_b + m_prev
            mt = jnp.maximum(mvec_scr[h * N_MVEC + 1, arow, :], log_inter)
            mn = jnp.maximum(b_last + m_prev, msc_scr[c, h * N_MSC + 1, 0:1, :])
            wk = jnp.exp(mvec_scr[h * N_MVEC + 2, arow, :] - mn)
            m_t.append(mt)
            w_inter.append(jnp.exp(log_inter - mt))
            s.append(_bf(mqk_scr[h, arow, :] * jnp.exp(mlogd_scr[h, arow, :] - mt[:, 0:chunk])))
            v = _bf(proj_scr[arow, P_MV + h * M_DV:P_MV + (h + 1) * M_DV])
            v_ext.append(jnp.concatenate([v, ones_lane0], axis=1))
            kw.append(_bf(mk_scr[h, arow, :] * wk[:, 0:M_DK]))
            c_decay.append(jnp.exp(b_last + m_prev - mn))
            m_new.append(mn)

        cn = [cn_scr[h] for h in range(M_HEADS)]
        st = [s_scr[h] for h in range(G_HEADS)]
        wq = [_dot(jnp.concatenate([gw_scr[h, arow, :], gqd_scr[h, arow, :]], axis=0), _bf(st[h]))
              for h in range(G_HEADS)]
        sv = [_dot(s[h], v_ext[h]) for h in range(M_HEADS)]
        qc = [_dot(mq_scr[h, arow, :], _bf(cn[h])) for h in range(M_HEADS)]
        upd = [_dot_tn(kw[h], v_ext[h]) for h in range(M_HEADS)]
        v_new = [_bf(gu_scr[h, arow, :] - wq[h][0:chunk]) for h in range(G_HEADS)]
        o_g = [wq[h][chunk:2 * chunk] + _dot(gattn_scr[h, arow, :], v_new[h]) for h in range(G_HEADS)]
        s_upd = [_dot_tn(gkd_scr[h, arow, :], v_new[h]) for h in range(G_HEADS)]

        for h in range(M_HEADS):
            wi = w_inter[h]
            nd = sv[h] + jnp.concatenate([wi, wi], axis=1) * qc[h]
            den = jnp.broadcast_to(nd[:, M_DV:M_DV + 1], (chunk, M_DV))
            hm = nd[:, 0:M_DV] / jnp.maximum(jnp.abs(den), jnp.exp(-m_t[h]))
            cd = c_decay[h]
            cn_scr[h] = jnp.concatenate([cd, cd], axis=1) * cn[h] + upd[h]
            m_scr[h] = jnp.broadcast_to(m_new[h], (HALO, GATE_LANES))
            gate = proj_scr[arow, P_MO + h * M_DV:P_MO + (h + 1) * M_DV]
            hn = hm * lax.rsqrt(jnp.mean(hm * hm, axis=-1, keepdims=True) + EPS)
            y_scr[arow, h * M_DV:(h + 1) * M_DV] = _bf(
                _sigmoid(gate) * hn * mn_ref[:, h * M_DV:(h + 1) * M_DV])

        for h in range(G_HEADS):
            s_scr[h] = ggl_scr[c, h, 0:1, :] * st[h] + s_upd[h]
            o = o_g[h]
            gate = proj_scr[arow, P_GG + h * G_DV:P_GG + (h + 1) * G_DV]
            on = o * lax.rsqrt(jnp.mean(o * o, axis=-1, keepdims=True) + EPS)
            y_scr[arow, M_V + h * G_DV:M_V + (h + 1) * G_DV] = _bf(
                on * gn_ref[:, h * G_DV:(h + 1) * G_DV] * _silu(gate))
        return carry

    def seq_trip(i, carry):
        for j in range(seq_unroll):
            seq_body(i * seq_unroll + j, carry)
        return carry

    lax.fori_loop(0, n_chunks // seq_unroll, seq_trip, 0)
    o_ref[0] = _dot(y_scr[...], wout_ref[...])
    _residual_postnorm(o_ref, x_ref, g_post_ref)


def _ab_pack_w_in(w_in):
    sizes = (M_QK, M_QK, M_V, M_V, M_HEADS, M_HEADS, G_QK, G_QK, G_V, G_V, G_HEADS, G_HEADS)
    offs = [0]
    for sz in sizes:
        offs.append(offs[-1] + sz)
    seg = [w_in[:, offs[i]:offs[i + 1]] for i in range(len(sizes))]
    m_q, m_k, m_v, m_o, m_i, m_f, g_q, g_k, g_v, g_gate, g_a, g_b = seg
    gates = jnp.concatenate([m_i, m_f, g_a, g_b], axis=1)
    gates = jnp.pad(gates, ((0, 0), (0, GATE_LANES - gates.shape[1])))
    return jnp.concatenate([m_q, m_k, g_q, g_k, g_v, m_v, m_o, g_gate, gates], axis=1).astype(BF16)


def _lane_vec(pieces):
    v = jnp.zeros((GATE_LANES,), F32)
    for off, vals in pieces:
        v = lax.dynamic_update_slice(v, vals.astype(F32), (off,))
    return v.reshape(1, GATE_LANES)


def _ab_mixer(x, g_pre, g_post, w_in, conv_m, m_gate_bias, m_norm, conv_g, g_a_log, g_dt_bias, g_norm, w_out,
              *, block=512):
    B, S, D = x.shape
    chunk = AB_CHUNK
    tb = min(block, S)
    n_chunks = tb // chunk
    group = min(AB_GROUP, n_chunks)
    seq_unroll = min(AB_SEQ_UNROLL, n_chunks)
    assert S % tb == 0 and tb % chunk == 0 and n_chunks % group == 0 and n_chunks % seq_unroll == 0
    tri3, masks = _ab_tables(chunk)
    gbias = _lane_vec([(LANE_I, m_gate_bias[0]), (LANE_F, m_gate_bias[1]), (LANE_A, g_dt_bias)])
    alog = _lane_vec([(LANE_A, g_a_log)])
    kern = functools.partial(_ab_kernel, chunk=chunk, group=group, seq_unroll=seq_unroll)
    mix = M_V + G_V
    return pl.pallas_call(
        kern,
        grid=(B, S // tb),
        in_specs=[
            pl.BlockSpec((1, tb, D), lambda b, s: (b, s, 0)),
            _resident((1, D)), _resident((1, D)),
            _resident((D, AB_COLS)), _resident(conv_m.shape), _resident(conv_g.shape),
            _resident((1, GATE_LANES)), _resident((1, GATE_LANES)),
            _resident((1, M_V)), _resident((1, G_V)), _resident((mix, D)),
            _resident(tri3.shape), _resident(masks.shape),
        ],
        out_specs=pl.BlockSpec((1, tb, D), lambda b, s: (b, s, 0)),
        out_shape=jax.ShapeDtypeStruct((B, S, D), F32),
        scratch_shapes=[
            pltpu.VMEM((tb, D), BF16),
            pltpu.VMEM((N_CONV_TILES, HALO + tb, LANES), F32),
            pltpu.VMEM((tb, PLAIN_COLS), F32),
            pltpu.VMEM((tb, ACT_COLS), F32),
            pltpu.VMEM((tb, mix), BF16),
            pltpu.VMEM((M_HEADS, tb, M_DK), BF16),
            pltpu.VMEM((M_HEADS, tb, M_DK), F32),
            pltpu.VMEM((M_HEADS, tb, chunk), F32),
            pltpu.VMEM((M_HEADS, tb, chunk), F32),
            pltpu.VMEM((M_HEADS * N_MVEC, tb, GATE_LANES), F32),
            pltpu.VMEM((n_chunks, M_HEADS * N_MSC, HALO, GATE_LANES), F32),
            pltpu.VMEM((G_HEADS, tb, G_DV), F32),
            pltpu.VMEM((G_HEADS, tb, G_DK), BF16),
            pltpu.VMEM((G_HEADS, tb, G_DK), BF16),
            pltpu.VMEM((G_HEADS, tb, G_DK), BF16),
            pltpu.VMEM((G_HEADS, tb, chunk), BF16),
            pltpu.VMEM((n_chunks, G_HEADS, HALO, GATE_LANES), F32),
            pltpu.VMEM((M_HEADS, M_DK, M_DV + 128), F32),
            pltpu.VMEM((M_HEADS, HALO, GATE_LANES), F32),
            pltpu.VMEM((G_HEADS, G_DK, G_DV), F32),
        ],
        compiler_params=pltpu.CompilerParams(
            dimension_semantics=("arbitrary", "arbitrary"), vmem_limit_bytes=VMEM_LIMIT_BYTES),
        name="ab_mixer",
    )(x, g_pre.reshape(1, D), g_post.reshape(1, D), _ab_pack_w_in(w_in), conv_m, conv_g, gbias, alog,
      m_norm.reshape(1, M_V), g_norm.reshape(1, G_V), w_out.astype(BF16),
      jnp.asarray(tri3, BF16), jnp.asarray(masks, F32))


def kernel(x, norm_gains, ab_w_in, ab_conv_m, ab_m_gate_bias, ab_m_norm, ab_conv_g, ab_g_a_log, ab_g_dt_bias, ab_g_norm, ab_w_out, c_w_in, c_lb_logits, c_norm, c_w_out, ffn_w_up, ffn_conv_w, ffn_conv_b, ffn_w_down):
    depth = norm_gains.shape[0]
    p = jax.nn.softmax(c_lb_logits.astype(F32), axis=0)
    lower_bounds = jnp.cumsum(p, axis=0) - p[0]
    c_heads = c_w_out.shape[1] // 128
    for layer in range(depth):
        j = layer // 2
        g = norm_gains[layer]
        if layer % 2 == 0:
            x = _ab_mixer(x, g[0], g[1], ab_w_in[j], ab_conv_m[j], ab_m_gate_bias[j], ab_m_norm[j],
                          ab_conv_g[j], ab_g_a_log[j], ab_g_dt_bias[j], ab_g_norm[j], ab_w_out[j])
        else:
            x = _hgrn2_mixer(x, g[0], g[1], c_w_in[j], lower_bounds[layer], c_norm[j], c_w_out[j],
                             heads=c_heads)
        x = _ffn(x, g[2], g[3], ffn_w_up[layer], ffn_conv_w[layer], ffn_conv_b[layer], ffn_w_down[layer])
    return x
```

```python
import functools

import jax
import jax.numpy as jnp
from jax import lax
from jax.experimental import pallas as pl
from jax.experimental.pallas import tpu as pltpu

F32 = jnp.float32
BF16 = jnp.bfloat16
EPS = 1e-6
LANES = 128
MXU_COLS = 256
PROJ_COL_SLAB = 512
HALO = 8
FFN_DOWN_LAG = 2
FFN_DOWN_PART = 4
FFN_CONV_K = 3
ROW_TILE = 128
VMEM_LIMIT_BYTES = 56 * 1024 * 1024


def _rms(x, gain):
    return x * lax.rsqrt(jnp.mean(x * x, axis=-1, keepdims=True) + EPS) * gain


def _prenorm_to(u_scr, x_ref, gain_ref):
    tb = x_ref.shape[1]
    rt = min(ROW_TILE, tb)
    for r0 in range(0, tb, rt):
        u_scr[r0:r0 + rt, :] = _rms(x_ref[0, r0:r0 + rt, :], gain_ref[...]).astype(BF16)


def _residual_postnorm(o_ref, x_ref, gain_ref):
    tb = x_ref.shape[1]
    rt = min(ROW_TILE, tb)
    for r0 in range(0, tb, rt):
        rows = slice(r0, r0 + rt)
        o_ref[0, rows, :] = x_ref[0, rows, :] + _rms(o_ref[0, rows, :], gain_ref[...])


def _sigmoid(x):
    return 0.5 * jnp.tanh(0.5 * x) + 0.5


def _silu(x):
    return x * _sigmoid(x)


def _log1p_exp_neg_abs(x):
    return jnp.log(1.0 + jnp.exp(-jnp.abs(x)))


def _resident(shape):
    nd = len(shape)
    return pl.BlockSpec(shape, lambda b, s: (0,) * nd, pipeline_mode=pl.Buffered(1))


def _ffn_kernel(x_ref, g_pre_ref, g_post_ref, wup_ref, cw_ref, cb_ref, wdn_ref, o_ref,
                u_scr, a_scr, h_scr, carry_scr, *, d_ff, slab):
    tb = x_ref.shape[1]

    @pl.when(pl.program_id(1) == 0)
    def _():
        carry_scr[...] = jnp.zeros_like(carry_scr)

    n_slab = d_ff // slab
    rt = min(ROW_TILE, tb)
    _prenorm_to(u_scr, x_ref, g_pre_ref)

    def up_half(buf, c0):
        h = jnp.dot(u_scr[...], wup_ref[:, c0:c0 + slab], preferred_element_type=F32)
        for lt in range(slab // LANES):
            cols = slice(c0 + lt * LANES, c0 + (lt + 1) * LANES)
            h_scr[buf, lt, 0:HALO, :] = carry_scr[:, cols]
            h_scr[buf, lt, HALO:HALO + tb, :] = h[:, lt * LANES:(lt + 1) * LANES]
            carry_scr[:, cols] = h_scr[buf, lt, tb:tb + HALO, :]

    def conv_tile(buf, lt, c0, r0):
        cols = slice(c0 + lt * LANES, c0 + (lt + 1) * LANES)
        w = cw_ref[:, cols]
        acc = cb_ref[:, cols]
        for j in range(FFN_CONV_K):
            r = HALO + r0 - (FFN_CONV_K - 1) + j
            acc = acc + w[j:j + 1] * h_scr[buf, lt, r:r + rt, :]
        return acc

    done = 0

    def down_part(lo, hi):
        part = jnp.dot(a_scr[:, lo * slab:hi * slab], wdn_ref[lo * slab:hi * slab, :],
                       preferred_element_type=F32)
        o_ref[0] = part if lo == 0 else o_ref[0] + part

    for j in range(n_slab):
        bg, bv = 2 * (j % 2), 2 * (j % 2) + 1
        up_half(bg, j * slab)
        up_half(bv, d_ff + j * slab)
        for lt in range(slab // LANES):
            for r0 in range(0, tb, rt):
                g = conv_tile(bg, lt, j * slab, r0)
                v = conv_tile(bv, lt, d_ff + j * slab, r0)
                a_scr[r0:r0 + rt, j * slab + lt * LANES:j * slab + (lt + 1) * LANES] = (
                    _silu(g) * v).astype(BF16)
        ready = j + 1 - FFN_DOWN_LAG
        if ready - done >= FFN_DOWN_PART or (j == n_slab - 1 and ready > done):
            down_part(done, ready)
            done = ready
    if done < n_slab:
        down_part(done, n_slab)
    _residual_postnorm(o_ref, x_ref, g_post_ref)


def _ffn(x, g_pre, g_post, w_up, conv_w, conv_b, w_down, *, block=1024, slab=MXU_COLS):
    B, S, D = x.shape
    d_ff = w_down.shape[0]
    tb = min(block, S)
    assert S % tb == 0 and d_ff % slab == 0
    kern = functools.partial(_ffn_kernel, d_ff=d_ff, slab=slab)
    return pl.pallas_call(
        kern,
        grid=(B, S // tb),
        in_specs=[
            pl.BlockSpec((1, tb, D), lambda b, s: (b, s, 0)),
            _resident((1, D)), _resident((1, D)),
            _resident((D, 2 * d_ff)), _resident((3, 2 * d_ff)), _resident((1, 2 * d_ff)),
            _resident((d_ff, D)),
        ],
        out_specs=pl.BlockSpec((1, tb, D), lambda b, s: (b, s, 0)),
        out_shape=jax.ShapeDtypeStruct((B, S, D), F32),
        scratch_shapes=[
            pltpu.VMEM((tb, D), BF16),
            pltpu.VMEM((tb, d_ff), BF16),
            pltpu.VMEM((4, slab // LANES, HALO + tb, LANES), F32),
            pltpu.VMEM((HALO, 2 * d_ff), F32),
        ],
        compiler_params=pltpu.CompilerParams(
            dimension_semantics=("arbitrary", "arbitrary"), vmem_limit_bytes=VMEM_LIMIT_BYTES),
        name="ffn",
    )(x, g_pre.reshape(1, D), g_post.reshape(1, D), w_up.astype(BF16), conv_w,
      conv_b.reshape(1, 2 * d_ff), w_down.astype(BF16))


def _dot(a, b):
    return jnp.dot(a, b, preferred_element_type=F32)


def _dot_nt(a, b):
    return lax.dot_general(a, b, (((1,), (1,)), ((), ())), preferred_element_type=F32)


def _dot_tn(a, b):
    return lax.dot_general(a, b, (((0,), (0,)), ((), ())), preferred_element_type=F32)


def _bf(x):
    return x.astype(BF16)


def _split3(x):
    hi = x.astype(BF16)
    r1 = x - hi.astype(F32)
    mid = r1.astype(BF16)
    lo = (r1 - mid.astype(F32)).astype(BF16)
    return jnp.concatenate([hi, mid, lo], axis=0)


def _log_sigmoid(x):
    return jnp.minimum(x, 0.0) - _log1p_exp_neg_abs(x)


def _level_sizes(chunk):
    sizes, m = [], chunk // 2
    while m >= 1:
        sizes.append(m)
        m //= 2
    return sizes


def _hgrn2_tables(chunk):
    import numpy as np
    L = chunk
    idx = np.arange(L)
    mats = [(idx[None, :] <= idx[:, None]).astype(np.float32)]
    masks = []
    for m in _level_sizes(L):
        in_b = (idx % (2 * m)) >= m
        blk_start = (idx // m) * m
        blk_end = blk_start + m - 1
        c = idx[None, :]
        mat_b = (c >= blk_start[:, None]) & (c <= idx[:, None])
        mat_a = (c > idx[:, None]) & (c <= blk_end[:, None])
        mats.append(np.where(in_b[:, None], mat_b, mat_a).astype(np.float32))
        same_pair = (idx[:, None] // (2 * m)) == (idx[None, :] // (2 * m))
        masks.append((same_pair & in_b[:, None] & (~in_b)[None, :]).astype(np.float32))
    masks.append(np.eye(L, dtype=np.float32))
    pair_masks = np.stack([np.kron(np.eye(2, dtype=np.float32), m) for m in masks])
    mstack = np.concatenate(mats, axis=0)
    return np.concatenate([mstack] * 3, axis=1), pair_masks


HGRN2_UNROLL = 2
HGRN2_COL_SLAB = 256
HGRN2_DK = 128
LOG_FLOOR = -1e30


def _hgrn2_kernel(x_ref, g_pre_ref, g_post_ref, win_ref, lb_ref, ng_ref, wout_ref, mstack_ref, mask_ref,
                  o_ref, u_scr, proj_scr, y_scr, q_scr, k_scr, xg_scr, e_scr, st_scr, *, heads, dk, chunk, unroll):
    tb = x_ref.shape[1]
    hk = heads * dk
    n_lvl = len(_level_sizes(chunk))

    @pl.when(pl.program_id(1) == 0)
    def _():
        st_scr[...] = jnp.zeros_like(st_scr)

    _prenorm_to(u_scr, x_ref, g_pre_ref)
    for j in range(4):
        proj_scr[:, j * hk:(j + 1) * hk] = _dot(u_scr[...], win_ref[:, j * hk:(j + 1) * hk])

    def decay_phase(c, buf):
        rows = pl.ds(pl.multiple_of(c * chunk, chunk), chunk)
        for c0 in range(0, hk, HGRN2_COL_SLAB):
            cs = slice(c0, c0 + HGRN2_COL_SLAB)
            ff = proj_scr[rows, hk + c0:hk + c0 + HGRN2_COL_SLAB]
            q_scr[buf, :, cs] = _bf(_silu(proj_scr[rows, cs]) * (dk ** -0.5))
            k = (1.0 - lb_ref[:, cs]) * (0.5 - 0.5 * jnp.tanh(0.5 * ff))
            k_scr[buf, :, cs] = _bf(k)
            lf3 = _split3(jnp.maximum(jnp.log(1.0 - k), LOG_FLOOR))
            tab = _dot(mstack_ref[...], lf3)
            xg_scr[buf, :, cs] = tab[0:chunk]
            e_scr[buf, :, cs] = _bf(jnp.exp(tab[chunk:]))

    def heads_phase(c, buf):
        rows = pl.ds(pl.multiple_of(c * chunk, chunk), chunk)
        pairs = [(h, h + 1) for h in range(0, heads, 2)]
        col = lambda h: slice(h * dk, (h + 1) * dk)
        stack = lambda f, p: jnp.concatenate([f(p[0]), f(p[1])], axis=0)
        q2 = [stack(lambda h: q_scr[buf, :, col(h)], p) for p in pairs]
        k2 = [stack(lambda h: k_scr[buf, :, col(h)], p) for p in pairs]
        attn = [None] * len(pairs)
        for i, p in enumerate(pairs):
            for lvl in range(n_lvl + 1):
                q, k = q2[i], k2[i]
                if lvl < n_lvl:
                    e = stack(lambda h: e_scr[buf, lvl * chunk:(lvl + 1) * chunk, col(h)], p)
                    q, k = q * e, k * e
                prod = _bf(_dot_nt(q, k)) * mask_ref[lvl]
                attn[i] = prod if lvl == 0 else attn[i] + prod
        hs = range(heads)
        g = [xg_scr[buf, :, col(h)] for h in hs]
        v = [proj_scr[rows, 2 * hk + h * dk:2 * hk + (h + 1) * dk].astype(BF16) for h in hs]
        st = [st_scr[h] for h in hs]
        o_inter = [_dot_nt(q_scr[buf, :, col(h)] * _bf(jnp.exp(g[h])), st[h].astype(BF16)) for h in hs]
        kv = []
        for h in hs:
            g_last = g[h][chunk - 1:chunk, :]
            k_dec = k_scr[buf, :, col(h)] * _bf(jnp.exp(g_last - g[h]))
            kv.append(_dot_tn(v[h], k_dec))
        o_intra = [_dot(attn[i], jnp.concatenate([v[p[0]], v[p[1]]], axis=0))
                   for i, p in enumerate(pairs)]
        for h in hs:
            st_scr[h] = st[h] * jnp.exp(g[h][chunk - 1:chunk, :]) + kv[h]
            o = o_inter[h] + o_intra[h // 2][(h % 2) * chunk:(h % 2 + 1) * chunk]
            gate = proj_scr[rows, 3 * hk + h * dk:3 * hk + (h + 1) * dk]
            y = (o * lax.rsqrt(jnp.mean(o * o, axis=-1, keepdims=True) + EPS) * ng_ref[:, col(h)]
                 * _silu(gate))
            y_scr[rows, col(h)] = y.astype(BF16)

    def chunks_body(i, carry):
        for j in range(unroll):
            decay_phase(i * unroll + j, j)
        for j in range(unroll):
            heads_phase(i * unroll + j, j)
        return carry

    lax.fori_loop(0, tb // chunk // unroll, chunks_body, 0)
    o_ref[0] = _dot(y_scr[...], wout_ref[...])
    _residual_postnorm(o_ref, x_ref, g_post_ref)


def _hgrn2_mixer(x, g_pre, g_post, w_in, lower_bound, norm_gain, w_out, *, heads, block=1024, chunk=64):
    B, S, D = x.shape
    hk = w_out.shape[0]
    dk = hk // heads
    tb = min(block, S)
    unroll = min(HGRN2_UNROLL, tb // chunk)
    assert S % tb == 0 and tb % (chunk * unroll) == 0 and w_in.shape[1] == 4 * hk
    mstack, masks = _hgrn2_tables(chunk)
    n_rows = mstack.shape[0]
    kern = functools.partial(_hgrn2_kernel, heads=heads, dk=dk, chunk=chunk, unroll=unroll)
    return pl.pallas_call(
        kern,
        grid=(B, S // tb),
        in_specs=[
            pl.BlockSpec((1, tb, D), lambda b, s: (b, s, 0)),
            _resident((1, D)), _resident((1, D)),
            _resident((D, 4 * hk)), _resident((1, hk)), _resident((1, hk)), _resident((hk, D)),
            _resident(mstack.shape), _resident(masks.shape),
        ],
        out_specs=pl.BlockSpec((1, tb, D), lambda b, s: (b, s, 0)),
        out_shape=jax.ShapeDtypeStruct((B, S, D), F32),
        scratch_shapes=[
            pltpu.VMEM((tb, D), BF16),
            pltpu.VMEM((tb, 4 * hk), F32),
            pltpu.VMEM((tb, hk), BF16),
            pltpu.VMEM((unroll, chunk, hk), BF16),
            pltpu.VMEM((unroll, chunk, hk), BF16),
            pltpu.VMEM((unroll, chunk, hk), F32),
            pltpu.VMEM((unroll, n_rows - chunk, hk), BF16),
            pltpu.VMEM((heads, dk, dk), F32),
        ],
        compiler_params=pltpu.CompilerParams(
            dimension_semantics=("arbitrary", "arbitrary"), vmem_limit_bytes=VMEM_LIMIT_BYTES),
        name="hgrn2_mixer",
    )(x, g_pre.reshape(1, D), g_post.reshape(1, D), w_in.astype(BF16), lower_bound.reshape(1, hk),
      norm_gain.reshape(1, hk), w_out.astype(BF16), jnp.asarray(mstack, BF16), jnp.asarray(masks, BF16))


M_HEADS, M_DK, M_DV = 4, 64, 128
G_HEADS, G_DK, G_DV = 4, 128, 128
M_QK = M_HEADS * M_DK
M_V = M_HEADS * M_DV
G_QK = G_HEADS * G_DK
G_V = G_HEADS * G_DV
AB_CHUNK = 64
AB_GROUP = 8
AB_SEQ_UNROLL = 4
CONV_K = 4
GATE_SOFTCAP = 15.0
GATE_LANES = LANES
CT_MQ = 0
CT_MK = CT_MQ + M_QK // LANES
CT_GQ = CT_MK + M_QK // LANES
CT_GK = CT_GQ + G_QK // LANES
CT_GV = CT_GK + G_QK // LANES
N_CONV_TILES = CT_GV + G_V // LANES
CONV_COLS = N_CONV_TILES * LANES
P_MV = 0
P_MO = P_MV + M_V
P_GG = P_MO + M_V
P_GATES = P_GG + G_V
PLAIN_COLS = P_GATES + GATE_LANES
AB_COLS = CONV_COLS + PLAIN_COLS
LANE_I, LANE_F, LANE_A, LANE_B = 0, M_HEADS, 2 * M_HEADS, 2 * M_HEADS + G_HEADS
ACT_MQ, ACT_MK = 0, M_QK
ACT_GQ = 2 * M_QK
ACT_GK = ACT_GQ + G_QK
ACT_GV = ACT_GK + G_QK
ACT_COLS = ACT_GV + G_V
N_MVEC = 3
N_MSC = 2


def _ab_tables(chunk):
    import numpy as np
    idx = np.arange(chunk)
    l, s = idx[:, None], idx[None, :]
    incl = l >= s
    strict = l > s
    blk16 = (l // 16) == (s // 16)
    blk32 = (l // 32) == (s // 32)
    masks = np.stack([incl, strict, blk16, blk32 & ~blk16, ~blk32, l == s]).astype(np.float32)
    tri3 = np.concatenate([incl.astype(np.float32)] * 3, axis=1)
    return tri3, masks


def _bcast_col(a, lane_ids, j):
    col = jnp.sum(jnp.where(lane_ids == j, a, 0.0), axis=-1, keepdims=True)
    return jnp.broadcast_to(col, a.shape)


def _unit_lower_inverses(ns, blk16, off32, off64, eye):
    d = [n * blk16 for n in ns]
    db = [_bf(x) for x in d]
    d2 = [_dot(x, x) for x in db]
    d2b = [_bf(x) for x in d2]
    p = [_dot(_bf(eye - a), _bf(eye + b)) for a, b in zip(d, d2)]
    d4 = [_dot(x, x) for x in d2b]
    d4b = [_bf(x) for x in d4]
    p = [_dot(_bf(a), _bf(eye + b)) for a, b in zip(p, d4)]
    d8 = [_dot(x, x) for x in d4b]
    p = [_dot(_bf(a), _bf(eye + b)) for a, b in zip(p, d8)]
    for off in (off32, off64):
        pb = [_bf(x) for x in p]
        t = [_dot(a, _bf(n * off)) for a, n in zip(pb, ns)]
        t = [_dot(_bf(a), b) for a, b in zip(t, pb)]
        p = [a - b for a, b in zip(p, t)]
    return p


def _ab_kernel(x_ref, g_pre_ref, g_post_ref, win_ref, cm_ref, cg_ref, gbias_ref, alog_ref, mn_ref, gn_ref,
               wout_ref, tri3_ref, mask_ref, o_ref,
               u_scr, conv_scr, proj_scr, act_scr, y_scr,
               mq_scr, mk_scr, mqk_scr, mlogd_scr, mvec_scr, msc_scr,
               gu_scr, gw_scr, gqd_scr, gkd_scr, gattn_scr, ggl_scr,
               cn_scr, m_scr, s_scr, *, chunk, group, seq_unroll):
    tb = x_ref.shape[1]
    n_chunks = tb // chunk

    @pl.when(pl.program_id(1) == 0)
    def _():
        conv_scr[:, 0:HALO, :] = jnp.zeros((N_CONV_TILES, HALO, LANES), F32)
        cn_scr[...] = jnp.zeros_like(cn_scr)
        m_scr[...] = jnp.zeros_like(m_scr)
        s_scr[...] = jnp.zeros_like(s_scr)

    _prenorm_to(u_scr, x_ref, g_pre_ref)
    for c0 in range(0, CONV_COLS, PROJ_COL_SLAB):
        r = _dot(u_scr[...], win_ref[:, c0:c0 + PROJ_COL_SLAB])
        for lt in range(PROJ_COL_SLAB // LANES):
            conv_scr[c0 // LANES + lt, HALO:HALO + tb, :] = r[:, lt * LANES:(lt + 1) * LANES]
    for c0 in range(0, PLAIN_COLS, PROJ_COL_SLAB):
        c1 = min(c0 + PROJ_COL_SLAB, PLAIN_COLS)
        proj_scr[:, c0:c1] = _dot(u_scr[...], win_ref[:, CONV_COLS + c0:CONV_COLS + c1])

    def conv_silu(ci, lt, w_ref, wcol0):
        r0 = HALO + ci * chunk - (CONV_K - 1)
        acc = None
        for j in range(CONV_K):
            t = w_ref[j:j + 1, wcol0:wcol0 + LANES] * conv_scr[lt, r0 + j:r0 + j + chunk, :]
            acc = t if acc is None else acc + t
        return _silu(acc)

    def unit(v):
        return v * lax.rsqrt(jnp.sum(v * v, axis=-1, keepdims=True) + EPS)

    for ci in range(n_chunks):
        arow = slice(ci * chunk, (ci + 1) * chunk)
        for t in range(M_QK // LANES):
            mq = conv_silu(ci, CT_MQ + t, cm_ref, t * LANES)
            act_scr[arow, ACT_MQ + t * LANES:ACT_MQ + (t + 1) * LANES] = mq * (M_DK ** -0.5)
            mk = conv_silu(ci, CT_MK + t, cm_ref, M_QK + t * LANES)
            act_scr[arow, ACT_MK + t * LANES:ACT_MK + (t + 1) * LANES] = mk
        for h in range(G_HEADS):
            gq = conv_silu(ci, CT_GQ + h, cg_ref, h * G_DK)
            act_scr[arow, ACT_GQ + h * G_DK:ACT_GQ + (h + 1) * G_DK] = unit(gq) * (G_DK ** -0.5)
            gk = conv_silu(ci, CT_GK + h, cg_ref, G_QK + h * G_DK)
            act_scr[arow, ACT_GK + h * G_DK:ACT_GK + (h + 1) * G_DK] = unit(gk)
            gv = conv_silu(ci, CT_GV + h, cg_ref, 2 * G_QK + h * G_DV)
            act_scr[arow, ACT_GV + h * G_DV:ACT_GV + (h + 1) * G_DV] = gv
    conv_scr[:, 0:HALO, :] = conv_scr[:, tb:tb + HALO, :]

    lane = lax.broadcasted_iota(jnp.int32, (chunk, GATE_LANES), 1)
    gbias = gbias_ref[...]
    neg_a = -jnp.exp(alog_ref[...])

    def prep_body(g, carry):
        m_incl, m_strict, m_blk16, m_off32, m_off64, m_eye = (mask_ref[i] for i in range(6))
        ns, beta_rows, bw_rows, arows = [], [], [], []
        for i in range(group):
            c = g * group + i
            a0 = pl.multiple_of(c * chunk, chunk)
            arow = pl.ds(a0, chunk)

            z = proj_scr[arow, P_GATES:P_GATES + GATE_LANES] + gbias
            capped = GATE_SOFTCAP * jnp.tanh(z / GATE_SOFTCAP)
            softplus = jnp.maximum(z, 0.0) + _log1p_exp_neg_abs(z)
            gp = jnp.where(lane < LANE_F, capped,
                           jnp.where(lane < LANE_A, _log_sigmoid(capped),
                                     jnp.where(lane < LANE_B, neg_a * softplus, _sigmoid(z))))
            cs = _dot(tri3_ref[...], _split3(gp))
            gp_t = gp.T
            cs_t = cs.T

            for h in range(M_HEADS):
                q = _bf(act_scr[arow, ACT_MQ + h * M_DK:ACT_MQ + (h + 1) * M_DK])
                k = act_scr[arow, ACT_MK + h * M_DK:ACT_MK + (h + 1) * M_DK]
                b_b = _bcast_col(cs, lane, LANE_F + h)
                ig_b = _bcast_col(gp, lane, LANE_I + h)
                b_row = cs_t[LANE_F + h:LANE_F + h + 1, :]
                ig_row = gp_t[LANE_I + h:LANE_I + h + 1, :]
                log_d = jnp.where(m_incl > 0.0, b_b[:, 0:chunk] - b_row + ig_row, -jnp.inf)
                b_last = b_b[chunk - 1:chunk, :]
                log_w = b_last - b_b + ig_b
                mq_scr[h, arow, :] = q
                mk_scr[h, arow, :] = k
                mqk_scr[h, arow, :] = _dot_nt(q, _bf(k))
                mlogd_scr[h, arow, :] = log_d
                mvec_scr[h * N_MVEC + 0, arow, :] = b_b
                mvec_scr[h * N_MVEC + 1, arow, :] = jnp.broadcast_to(
                    jnp.max(log_d, axis=-1, keepdims=True), (chunk, GATE_LANES))
                mvec_scr[h * N_MVEC + 2, arow, :] = log_w
                msc_scr[c, h * N_MSC + 0] = jnp.broadcast_to(b_last, (HALO, GATE_LANES))
                msc_scr[c, h * N_MSC + 1] = jnp.broadcast_to(
                    jnp.max(log_w, axis=0, keepdims=True), (HALO, GATE_LANES))

            for h in range(G_HEADS):
                q = act_scr[arow, ACT_GQ + h * G_DK:ACT_GQ + (h + 1) * G_DK]
                k = act_scr[arow, ACT_GK + h * G_DK:ACT_GK + (h + 1) * G_DK]
                g_b = _bcast_col(cs, lane, LANE_A + h)
                beta = _bcast_col(gp, lane, LANE_B + h)
                g_row = cs_t[LANE_A + h:LANE_A + h + 1, :]
                decay = jnp.where(m_incl > 0.0,
                                  jnp.exp(jnp.where(m_incl > 0.0, g_b[:, 0:chunk] - g_row, 0.0)), 0.0)
                kb = _bf(k)
                ns.append(m_strict * (beta[:, 0:chunk] * _dot_nt(kb, kb) * decay))
                gattn_scr[h, arow, :] = _bf(_dot_nt(_bf(q), kb) * decay)
                g_last = g_b[chunk - 1:chunk, :]
                gqd_scr[h, arow, :] = _bf(q * jnp.exp(g_b))
                gkd_scr[h, arow, :] = _bf(k * jnp.exp(g_last - g_b))
                ggl_scr[c, h] = jnp.broadcast_to(jnp.exp(g_last), (HALO, GATE_LANES))
                beta_row = gp_t[LANE_B + h:LANE_B + h + 1, :]
                beta_rows.append(beta_row)
                bw_rows.append(beta_row * jnp.exp(g_row))
                arows.append((h, arow))

        t_inv = _unit_lower_inverses(ns, m_blk16, m_off32, m_off64, m_eye)
        for t, beta_row, bw_row, (h, arow) in zip(t_inv, beta_rows, bw_rows, arows):
            k = act_scr[arow, ACT_GK + h * G_DK:ACT_GK + (h + 1) * G_DK]
            v = act_scr[arow, ACT_GV + h * G_DV:ACT_GV + (h + 1) * G_DV]
            gu_scr[h, arow, :] = _dot(_bf(t * beta_row), _bf(v))
            gw_scr[h, arow, :] = _bf(_dot(_bf(t * bw_row), _bf(k)))
        return carry

    lax.fori_loop(0, n_chunks // group, prep_body, 0)

    ones_lane0 = _bf(jnp.where(lax.broadcasted_iota(jnp.int32, (chunk, M_DV), 1) == 0, 1.0, 0.0))

    def seq_body(c, carry):
        a0 = pl.multiple_of(c * chunk, chunk)
        arow = pl.ds(a0, chunk)

        st = [s_scr[h] for h in range(G_HEADS)]
        wq = [_dot(jnp.concatenate([gw_scr[h, arow, :], gqd_scr[h, arow, :]], axis=0), _bf(st[h]))
              for h in range(G_HEADS)]

        for h in range(M_HEADS):
            m_prev = m_scr[h, 0:1, :]
            b_b = mvec_scr[h * N_MVEC + 0, arow, :]
            b_last = msc_scr[c, h * N_MSC + 0, 0:1, :]
            log_inter = b_b + m_prev
            m_t = jnp.maximum(mvec_scr[h * N_MVEC + 1, arow, :], log_inter)
            m_new = jnp.maximum(b_last + m_prev, msc_scr[c, h * N_MSC + 1, 0:1, :])
            wk = jnp.exp(mvec_scr[h * N_MVEC + 2, arow, :] - m_new)
            w_inter = jnp.exp(log_inter - m_t)
            s = _bf(mqk_scr[h, arow, :] * jnp.exp(mlogd_scr[h, arow, :] - m_t[:, 0:chunk]))
            v = _bf(proj_scr[arow, P_MV + h * M_DV:P_MV + (h + 1) * M_DV])
            v_ext = jnp.concatenate([v, ones_lane0], axis=1)
            kw = _bf(mk_scr[h, arow, :] * wk[:, 0:M_DK])
            cn = cn_scr[h]
            sv = _dot(s, v_ext)
            qc = _dot(mq_scr[h, arow, :], _bf(cn))
            upd = _dot_tn(kw, v_ext)
            nd = sv + jnp.concatenate([w_inter, w_inter], axis=1) * qc
            den = jnp.broadcast_to(nd[:, M_DV:M_DV + 1], (chunk, M_DV))
            hm = nd[:, 0:M_DV] / jnp.maximum(jnp.abs(den), jnp.exp(-m_t))
            cd = jnp.exp(b_last + m_prev - m_new)
            cn_scr[h] = jnp.concatenate([cd, cd], axis=1) * cn + upd
            m_scr[h] = jnp.broadcast_to(m_new, (HALO, GATE_LANES))
            gate = proj_scr[arow, P_MO + h * M_DV:P_MO + (h + 1) * M_DV]
            hn = hm * lax.rsqrt(jnp.mean(hm * hm, axis=-1, keepdims=True) + EPS)
            y_scr[arow, h * M_DV:(h + 1) * M_DV] = _bf(
                _sigmoid(gate) * hn * mn_ref[:, h * M_DV:(h + 1) * M_DV])

        v_new = [_bf(gu_scr[h, arow, :] - wq[h][0:chunk]) for h in range(G_HEADS)]
        for h in range(G_HEADS):
            o = wq[h][chunk:2 * chunk] + _dot(gattn_scr[h, arow, :], v_new[h])
            s_scr[h] = ggl_scr[c, h, 0:1, :] * st[h] + _dot_tn(gkd_scr[h, arow, :], v_new[h])
            gate = proj_scr[arow, P_GG + h * G_DV:P_GG + (h + 1) * G_DV]
            on = o * lax.rsqrt(jnp.mean(o * o, axis=-1, keepdims=True) + EPS)
            y_scr[arow, M_V + h * G_DV:M_V + (h + 1) * G_DV] = _bf(
                on * gn_ref[:, h * G_DV:(h + 1) * G_DV] * _silu(gate))
        return carry

    def seq_trip(i, carry):
        for j in range(seq_unroll):
            seq_body(i * seq_unroll + j, carry)
        return carry

    lax.fori_loop(0, n_chunks // seq_unroll, seq_trip, 0)
    o_ref[0] = _dot(y_scr[...], wout_ref[...])
    _residual_postnorm(o_ref, x_ref, g_post_ref)


def _ab_pack_w_in(w_in):
    sizes = (M_QK, M_QK, M_V, M_V, M_HEADS, M_HEADS, G_QK, G_QK, G_V, G_V, G_HEADS, G_HEADS)
    offs = [0]
    for sz in sizes:
        offs.append(offs[-1] + sz)
    seg = [w_in[:, offs[i]:offs[i + 1]] for i in range(len(sizes))]
    m_q, m_k, m_v, m_o, m_i, m_f, g_q, g_k, g_v, g_gate, g_a, g_b = seg
    gates = jnp.concatenate([m_i, m_f, g_a, g_b], axis=1)
    gates = jnp.pad(gates, ((0, 0), (0, GATE_LANES - gates.shape[1])))
    return jnp.concatenate([m_q, m_k, g_q, g_k, g_v, m_v, m_o, g_gate, gates], axis=1).astype(BF16)


def _lane_vec(pieces):
    parts, pos = [], 0
    for off, vals in pieces:
        parts += [jnp.zeros((off - pos,), F32), vals.astype(F32)]
        pos = off + vals.shape[0]
    parts.append(jnp.zeros((GATE_LANES - pos,), F32))
    return jnp.concatenate(parts).reshape(1, GATE_LANES)


def _ab_mixer(x, g_pre, g_post, w_in, conv_m, m_gate_bias, m_norm, conv_g, g_a_log, g_dt_bias, g_norm, w_out,
              *, block=512):
    B, S, D = x.shape
    chunk = AB_CHUNK
    tb = min(block, S)
    n_chunks = tb // chunk
    group = min(AB_GROUP, n_chunks)
    seq_unroll = min(AB_SEQ_UNROLL, n_chunks)
    assert S % tb == 0 and tb % chunk == 0 and n_chunks % group == 0 and n_chunks % seq_unroll == 0
    tri3, masks = _ab_tables(chunk)
    gbias = _lane_vec([(LANE_I, m_gate_bias[0]), (LANE_F, m_gate_bias[1]), (LANE_A, g_dt_bias)])
    alog = _lane_vec([(LANE_A, g_a_log)])
    kern = functools.partial(_ab_kernel, chunk=chunk, group=group, seq_unroll=seq_unroll)
    mix = M_V + G_V
    return pl.pallas_call(
        kern,
        grid=(B, S // tb),
        in_specs=[
            pl.BlockSpec((1, tb, D), lambda b, s: (b, s, 0)),
            _resident((1, D)), _resident((1, D)),
            _resident((D, AB_COLS)), _resident(conv_m.shape), _resident(conv_g.shape),
            _resident((1, GATE_LANES)), _resident((1, GATE_LANES)),
            _resident((1, M_V)), _resident((1, G_V)), _resident((mix, D)),
            _resident(tri3.shape), _resident(masks.shape),
        ],
        out_specs=pl.BlockSpec((1, tb, D), lambda b, s: (b, s, 0)),
        out_shape=jax.ShapeDtypeStruct((B, S, D), F32),
        scratch_shapes=[
            pltpu.VMEM((tb, D), BF16),
            pltpu.VMEM((N_CONV_TILES, HALO + tb, LANES), F32),
            pltpu.VMEM((tb, PLAIN_COLS), F32),
            pltpu.VMEM((tb, ACT_COLS), F32),
            pltpu.VMEM((tb, mix), BF16),
            pltpu.VMEM((M_HEADS, tb, M_DK), BF16),
            pltpu.VMEM((M_HEADS, tb, M_DK), F32),
            pltpu.VMEM((M_HEADS, tb, chunk), F32),
            pltpu.VMEM((M_HEADS, tb, chunk), F32),
            pltpu.VMEM((M_HEADS * N_MVEC, tb, GATE_LANES), F32),
            pltpu.VMEM((n_chunks, M_HEADS * N_MSC, HALO, GATE_LANES), F32),
            pltpu.VMEM((G_HEADS, tb, G_DV), F32),
            pltpu.VMEM((G_HEADS, tb, G_DK), BF16),
            pltpu.VMEM((G_HEADS, tb, G_DK), BF16),
            pltpu.VMEM((G_HEADS, tb, G_DK), BF16),
            pltpu.VMEM((G_HEADS, tb, chunk), BF16),
            pltpu.VMEM((n_chunks, G_HEADS, HALO, GATE_LANES), F32),
            pltpu.VMEM((M_HEADS, M_DK, M_DV + LANES), F32),
            pltpu.VMEM((M_HEADS, HALO, GATE_LANES), F32),
            pltpu.VMEM((G_HEADS, G_DK, G_DV), F32),
        ],
        compiler_params=pltpu.CompilerParams(
            dimension_semantics=("arbitrary", "arbitrary"), vmem_limit_bytes=VMEM_LIMIT_BYTES),
        name="ab_mixer",
    )(x, g_pre.reshape(1, D), g_post.reshape(1, D), _ab_pack_w_in(w_in), conv_m, conv_g, gbias, alog,
      m_norm.reshape(1, M_V), g_norm.reshape(1, G_V), w_out.astype(BF16),
      jnp.asarray(tri3, BF16), jnp.asarray(masks, F32))


def kernel(x, norm_gains, ab_w_in, ab_conv_m, ab_m_gate_bias, ab_m_norm, ab_conv_g, ab_g_a_log, ab_g_dt_bias, ab_g_norm, ab_w_out, c_w_in, c_lb_logits, c_norm, c_w_out, ffn_w_up, ffn_conv_w, ffn_conv_b, ffn_w_down):
    depth = norm_gains.shape[0]
    p = jax.nn.softmax(c_lb_logits.astype(F32), axis=0)
    lower_bounds = jnp.cumsum(p, axis=0) - p[0]
    c_heads = c_w_out.shape[1] // HGRN2_DK
    for layer in range(depth):
        j = layer // 2
        g = norm_gains[layer]
        if layer % 2 == 0:
            x = _ab_mixer(x, g[0], g[1], ab_w_in[j], ab_conv_m[j], ab_m_gate_bias[j], ab_m_norm[j],
                          ab_conv_g[j], ab_g_a_log[j], ab_g_dt_bias[j], ab_g_norm[j], ab_w_out[j])
        else:
            x = _hgrn2_mixer(x, g[0], g[1], c_w_in[j], lower_bounds[layer], c_norm[j], c_w_out[j],
                             heads=c_heads)
        x = _ffn(x, g[2], g[3], ffn_w_up[layer], ffn_conv_w[layer], ffn_conv_b[layer], ffn_w_down[layer])
    return x
```
